```python
import math
import jax
import jax.numpy as jnp
from jax import lax
import numpy as np

D_MODEL = 1024
BATCH = 2
SEQ = 16384
DEPTH = 4

GRID_W = 64
CTX_LEN = 256
D_FF = 2816
H_RET = 4
RET_DV = D_MODEL // H_RET
RET_DK = RET_DV // 2
RET_CHUNK = 128
H_DIFF = 8
DIFF_DV = D_MODEL // H_DIFF
DIFF_DH = DIFF_DV // 2
Q_BLOCK = 128
ROPE_BASE = 10000.0
RET_DECAY_EXP0 = 5.0
EPS = 1e-6
N_MOD = 9
SPLIT_SIZES = (H_RET * RET_DK, H_RET * RET_DK, H_RET * RET_DV, H_RET * RET_DV,
               H_DIFF * 2 * DIFF_DH, H_DIFF * 2 * DIFF_DH, H_DIFF * DIFF_DV, 2 * D_MODEL)
SPLIT_POINTS = tuple(int(v) for v in np.cumsum(SPLIT_SIZES)[:-1])
IN_WIDTH = int(sum(SPLIT_SIZES))

kernel_name = "hybrid_retention_diffattn_dit_block"


def rmsnorm(x, w):
    xf = x.astype(jnp.float32)
    y = xf * lax.rsqrt(jnp.mean(xf * xf, axis=-1, keepdims=True) + EPS)
    return (y * w.astype(jnp.float32)).astype(x.dtype)


def groupnorm_heads(y, w):
    yf = y.astype(jnp.float32)
    mu = jnp.mean(yf, axis=-1, keepdims=True)
    var = jnp.mean(jnp.square(yf - mu), axis=-1, keepdims=True)
    return ((yf - mu) * lax.rsqrt(var + EPS) * w.astype(jnp.float32)).astype(y.dtype)


def modulate(x, g, shift, scale):
    return rmsnorm(x, g) * (1.0 + scale) + shift


def add_residual(x, y, g_post, gate, weight):
    return x + weight * gate * rmsnorm(y, g_post)


def swiglu(h, w_in, w_out):
    a, b = jnp.split(h @ w_in, 2, axis=-1)
    return (jax.nn.silu(a) * b) @ w_out


def rope(x, pos):
    half = x.shape[-1] // 2
    inv_freq = ROPE_BASE ** (-jnp.arange(half, dtype=jnp.float32) / half)
    ang = pos.astype(jnp.float32)[:, None] * inv_freq[None, :]
    cos = jnp.cos(ang).astype(x.dtype)
    sin = jnp.sin(ang).astype(x.dtype)
    x1, x2 = x[..., :half], x[..., half:]
    return jnp.concatenate([x1 * cos - x2 * sin, x1 * sin + x2 * cos], axis=-1)


def rope_2d(x, row, col):
    h = x.shape[-1] // 2
    return jnp.concatenate([rope(x[..., :h], row), rope(x[..., h:], col)], axis=-1)


def to_heads(t, n_heads):
    return jnp.transpose(t.reshape(t.shape[0], t.shape[1], n_heads, -1), (0, 2, 1, 3))


def to_diff_heads(t):
    return jnp.transpose(t.reshape(t.shape[0], t.shape[1], H_DIFF, 2, DIFF_DH), (0, 2, 3, 1, 4))


def retention_dir(q, k, v, log_gamma, state0):
    B, H, T, dk = k.shape
    dv = v.shape[-1]
    n = T // RET_CHUNK
    kc = k.astype(jnp.float32).reshape(B, H, n, RET_CHUNK, dk)
    vc = v.astype(jnp.float32).reshape(B, H, n, RET_CHUNK, dv)
    pos = jnp.arange(RET_CHUNK, dtype=jnp.float32)
    zeta = jnp.exp((RET_CHUNK - 1.0 - pos)[None, :] * log_gamma[:, None])
    kv = jnp.einsum('bhncd,bhnce->bhnde', kc * zeta[None, :, None, :, None], vc)
    g_chunk = jnp.exp(RET_CHUNK * log_gamma)[None, :, None, None]

    def step(r, kv_i):
        return g_chunk * r + kv_i, r

    final, before = lax.scan(step, state0.astype(jnp.float32), jnp.moveaxis(kv, 2, 0))
    if q is None:
        return None, final
    before = jnp.moveaxis(before, 0, 2)
    qc = q.astype(jnp.float32).reshape(B, H, n, RET_CHUNK, dk)
    diff = pos[:, None] - pos[None, :]
    dmat = jnp.where(diff >= 0.0, jnp.exp(jnp.maximum(diff, 0.0)[None] * log_gamma[:, None, None]), 0.0)
    scores = jnp.einsum('bhncd,bhnmd->bhncm', qc, kc) * dmat[None, :, None]
    inner = jnp.einsum('bhncm,bhnme->bhnce', scores, vc)
    xi = jnp.exp((pos + 1.0)[None, :] * log_gamma[:, None])
    cross = jnp.einsum('bhncd,bhnde->bhnce', qc, before) * xi[None, :, None, :, None]
    return (inner + cross).reshape(B, H, T, dv).astype(v.dtype), final


def diff_attend(q, k, v, lam):
    s = jnp.einsum('bhiqd,bhikd->bhiqk', q, k).astype(jnp.float32) * (DIFF_DH ** -0.5)
    p = jax.nn.softmax(s, axis=-1)
    a = p[:, :, 0] - lam * p[:, :, 1]
    return jnp.einsum('bhqk,bhke->bhqe', a.astype(v.dtype), v)


def diff_attend_blocked(q, k, v, lam):
    B, H, _, T, d = q.shape
    nb = T // Q_BLOCK
    qb = jnp.moveaxis(q.reshape(B, H, 2, nb, Q_BLOCK, d), 3, 0)
    out = lax.map(lambda qi: diff_attend(qi, k, v, lam), qb)
    return jnp.moveaxis(out, 0, 2).reshape(B, H, T, -1)


def retention_output(y, g, gn_w):
    y = groupnorm_heads(jnp.transpose(y, (0, 2, 1, 3)), gn_w.reshape(H_RET, RET_DV))
    return y.reshape(g.shape) * jax.nn.silu(g)


def diff_output(o, subln_w, lam_init):
    o = rmsnorm(jnp.transpose(o, (0, 2, 1, 3)), subln_w) * (1.0 - lam_init)
    return o.reshape(o.shape[0], o.shape[1], -1)


def branch_merge(ret_out, diff_out, gates, w_out):
    g_a, g_b = jnp.split(jax.nn.sigmoid(gates), 2, axis=-1)
    return (g_a * ret_out + g_b * diff_out) @ w_out


def token_mixer(h, hc, w_in, w_out, decay_logit, gn_w, lam_vec, subln_w, lam_init, row, col, tpos, need_ctx_out):
    B = h.shape[0]
    rq, rk, rv, rg, dq, dk, dv, gates = jnp.split(h @ w_in, SPLIT_POINTS, axis=-1)
    rqc, rkc, rvc, rgc, dqc, dkc, dvc, gates_c = jnp.split(hc @ w_in, SPLIT_POINTS, axis=-1)
    flip = lambda t: jnp.flip(t, axis=2)

    log_gamma = jax.nn.log_sigmoid(decay_logit.astype(jnp.float32))
    q_r = rope(to_heads(rq, H_RET), tpos)
    k_r = rope(to_heads(rk, H_RET), tpos) * (RET_DK ** -0.5)
    v_r = to_heads(rv, H_RET)
    q_rc = to_heads(rqc, H_RET) if need_ctx_out else None
    k_rc = to_heads(rkc, H_RET) * (RET_DK ** -0.5)
    v_rc = to_heads(rvc, H_RET)
    zero = jnp.zeros((B, H_RET, RET_DK, RET_DV), jnp.float32)
    yc_f, s_f = retention_dir(q_rc, k_rc, v_rc, log_gamma[0], zero)
    yc_b, s_b = retention_dir(None if q_rc is None else flip(q_rc), flip(k_rc), flip(v_rc), log_gamma[1], zero)
    y_f, _ = retention_dir(q_r, k_r, v_r, log_gamma[0], s_f)
    y_b, _ = retention_dir(flip(q_r), flip(k_r), flip(v_r), log_gamma[1], s_b)
    ret_out = retention_output(y_f + flip(y_b), rg, gn_w)

    lv = lam_vec.astype(jnp.float32)
    lam = jnp.exp(jnp.sum(lv[0] * lv[1])) - jnp.exp(jnp.sum(lv[2] * lv[3])) + lam_init
    q_d = rope_2d(to_diff_heads(dq), row, col)
    k_d = rope_2d(to_diff_heads(dk), row, col)
    v_d = to_heads(dv, H_DIFF)
    k_dc = to_diff_heads(dkc)
    v_dc = to_heads(dvc, H_DIFF)
    k_all = jnp.concatenate([k_dc, k_d], axis=3)
    v_all = jnp.concatenate([v_dc, v_d], axis=2)
    diff_out = diff_output(diff_attend_blocked(q_d, k_all, v_all, lam), subln_w, lam_init)

    y = branch_merge(ret_out, diff_out, gates, w_out)
    if not need_ctx_out:
        return y, None
    ret_out_c = retention_output(yc_f + flip(yc_b), rgc, gn_w)
    diff_out_c = diff_output(diff_attend(to_diff_heads(dqc), k_dc, v_dc, lam), subln_w, lam_init)
    yc = branch_merge(ret_out_c, diff_out_c, gates_c, w_out)
    return y, yc


def setup_inputs(seed: int = 0) -> dict:
    key = jax.random.key(seed)
    ks = jax.random.split(key, 15)
    f32 = jnp.float32
    nrm = lambda k, shape, s: s * jax.random.normal(k, shape, f32)
    decay0 = jnp.log(2.0 ** (RET_DECAY_EXP0 + jnp.arange(H_RET, dtype=f32)) - 1.0)
    return {
        "x": nrm(ks[0], (BATCH, SEQ, D_MODEL), 1.0),
        "c": nrm(ks[1], (BATCH, D_MODEL), 1.0),
        "ctx": nrm(ks[2], (BATCH, CTX_LEN, D_MODEL), 1.0),
        "c_ctx": nrm(ks[3], (D_MODEL,), 1.0),
        "w_ada": nrm(ks[4], (DEPTH, D_MODEL, N_MOD * D_MODEL), D_MODEL ** -0.5),
        "b_ada": nrm(ks[5], (DEPTH, N_MOD * D_MODEL), 0.02),
        "norm_w": 1.0 + nrm(ks[6], (DEPTH, 6, D_MODEL), 0.02),
        "ffn_w_in": nrm(ks[7], (DEPTH, 2, D_MODEL, 2 * D_FF), D_MODEL ** -0.5),
        "ffn_w_out": nrm(ks[8], (DEPTH, 2, D_FF, D_MODEL), D_FF ** -0.5),
        "w_in": nrm(ks[9], (DEPTH, D_MODEL, IN_WIDTH), D_MODEL ** -0.5),
        "w_out": nrm(ks[10], (DEPTH, D_MODEL, D_MODEL), D_MODEL ** -0.5),
        "ret_decay_logit": decay0[None, None, :] + nrm(ks[11], (DEPTH, 2, H_RET), 0.1),
        "ret_gn_w": 1.0 + nrm(ks[12], (DEPTH, H_RET * RET_DV), 0.02),
        "diff_lambda": nrm(ks[13], (DEPTH, 4, DIFF_DH), 0.1),
        "diff_subln_w": 1.0 + nrm(ks[14], (DEPTH, DIFF_DV), 0.02),
    }


def reference(x, c, ctx, c_ctx, w_ada, b_ada, norm_w, ffn_w_in, ffn_w_out, w_in, w_out,
              ret_decay_logit, ret_gn_w, diff_lambda, diff_subln_w):
    B, T, D = x.shape
    rows = T // GRID_W
    row = jnp.broadcast_to(jnp.arange(rows, dtype=jnp.int32)[:, None], (rows, GRID_W)).reshape(-1)
    col = jnp.broadcast_to(jnp.arange(GRID_W, dtype=jnp.int32)[None, :], (rows, GRID_W)).reshape(-1)
    tpos = jnp.arange(rows * GRID_W, dtype=jnp.int32)
    xc = ctx
    for l in range(DEPTH):
        last = l == DEPTH - 1
        lam_init = 0.8 - 0.6 * math.exp(-0.3 * l)
        mod = (jax.nn.silu(c) @ w_ada[l] + b_ada[l]).reshape(B, N_MOD, 1, D)
        mod_c = (jax.nn.silu(c_ctx) @ w_ada[l] + b_ada[l]).reshape(N_MOD, D)
        nw = norm_w[l]
        x = add_residual(x, swiglu(modulate(x, nw[0], mod[:, 0], mod[:, 1]), ffn_w_in[l, 0], ffn_w_out[l, 0]),
                         nw[1], mod[:, 2], 0.5)
        xc = add_residual(xc, swiglu(modulate(xc, nw[0], mod_c[0], mod_c[1]), ffn_w_in[l, 0], ffn_w_out[l, 0]),
                          nw[1], mod_c[2], 0.5)
        h = modulate(x, nw[2], mod[:, 3], mod[:, 4])
        hc = modulate(xc, nw[2], mod_c[3], mod_c[4])
        y, yc = token_mixer(h, hc, w_in[l], w_out[l], ret_decay_logit[l], ret_gn_w[l], diff_lambda[l],
                            diff_subln_w[l], lam_init, row, col, tpos, not last)
        x = add_residual(x, y, nw[3], mod[:, 5], 1.0)
        x = add_residual(x, swiglu(modulate(x, nw[4], mod[:, 6], mod[:, 7]), ffn_w_in[l, 1], ffn_w_out[l, 1]),
                         nw[5], mod[:, 8], 0.5)
        if not last:
            xc = add_residual(xc, yc, nw[3], mod_c[5], 1.0)
            xc = add_residual(xc, swiglu(modulate(xc, nw[4], mod_c[6], mod_c[7]), ffn_w_in[l, 1], ffn_w_out[l, 1]),
                              nw[5], mod_c[8], 0.5)
    return x
```

```python
import functools
import math

import jax
import jax.numpy as jnp
from jax import lax
from jax.experimental import pallas as pl
from jax.experimental.pallas import tpu as pltpu

D_MODEL = 1024
D_FF = 2816
H_RET = 4
RET_DK = 128
RET_DV = 256
H_DIFF = 8
DIFF_DV = 128
DIFF_DH = 64
GRID_W = 64
ROPE_BASE = 10000.0
EPS = 1e-6
N_MOD = 9
IN_WIDTH = 8192
OFF_RQ, OFF_RK, OFF_RV, OFF_RG, OFF_DQ, OFF_DK, OFF_DV, OFF_GT = 0, 512, 1024, 2048, 3072, 4096, 5120, 6144

TM = 256
RET_BLOCK = 256
ATT_TQ = 512
ATT_TK = 640
VMEM_LIMIT = 56 * 1024 * 1024

BF16 = jnp.bfloat16
F32 = jnp.float32


def _cparams(sem):
    return pltpu.CompilerParams(dimension_semantics=sem, vmem_limit_bytes=VMEM_LIMIT)


def _resident(shape, index_map):
    return pl.BlockSpec(shape, index_map, pipeline_mode=pl.Buffered(1))


def _rms(x, g):
    return x * lax.rsqrt(jnp.mean(x * x, axis=-1, keepdims=True) + EPS) * g


def _mm(a, b):
    return jnp.dot(a, b, preferred_element_type=F32)


def _mm_nt(a, b):
    return lax.dot_general(a, b, (((1,), (1,)), ((), ())), preferred_element_type=F32)


def _mm_tn(a, b):
    return lax.dot_general(a, b, (((0,), (0,)), ((), ())), preferred_element_type=F32)


def _adaln_kernel(c_ref, w_ref, b_ref, o_ref):
    c = c_ref[...]
    s = c * jax.nn.sigmoid(c)
    o_ref[...] = _mm(s, w_ref[...]) + b_ref[...]


def _adaln(cs, w_ada, b_ada):
    depth = w_ada.shape[0]
    width = w_ada.shape[2]
    tn = 1024
    return pl.pallas_call(
        _adaln_kernel,
        grid=(depth, width // tn),
        in_specs=[
            pl.BlockSpec((8, D_MODEL), lambda l, j: (0, 0)),
            pl.BlockSpec((None, D_MODEL, tn), lambda l, j: (l, 0, j)),
            pl.BlockSpec((None, 1, tn), lambda l, j: (l, 0, j)),
        ],
        out_specs=pl.BlockSpec((None, 8, tn), lambda l, j: (l, 0, j)),
        out_shape=jax.ShapeDtypeStruct((depth, 8, width), F32),
        compiler_params=_cparams(("arbitrary", "arbitrary")),
        name="adaln",
    )(cs, w_ada, b_ada.reshape(depth, 1, width))


def _ffn_kernel(x_ref, mod_ref, nw_ref, win_ref, wout_ref, o_ref, *, mod_off, nw_off):
    x = x_ref[...]
    shift = mod_ref[mod_off:mod_off + 1, :]
    scale = mod_ref[mod_off + 1:mod_off + 2, :]
    gate = mod_ref[mod_off + 2:mod_off + 3, :]
    h = _rms(x, nw_ref[nw_off:nw_off + 1, :]) * (1.0 + scale) + shift
    hb = h.astype(BF16)
    a = _mm(hb, win_ref[:, :D_FF])
    b = _mm(hb, win_ref[:, D_FF:])
    u = (a * jax.nn.sigmoid(a) * b).astype(BF16)
    y = _mm(u, wout_ref[...])
    o_ref[...] = x + 0.5 * gate * _rms(y, nw_ref[nw_off + 1:nw_off + 2, :])


def _ffn(xa, modsel, nw, w_in, w_out, *, mod_off, nw_off, n_tiles, ctx_tile):
    bsz = xa.shape[0]
    kern = functools.partial(_ffn_kernel, mod_off=mod_off, nw_off=nw_off)
    return pl.pallas_call(
        kern,
        grid=(bsz, n_tiles),
        in_specs=[
            pl.BlockSpec((None, TM, D_MODEL), lambda b, t: (b, t, 0)),
            pl.BlockSpec((None, None, N_MOD, D_MODEL), lambda b, t: (b, t // ctx_tile, 0, 0)),
            _resident((6, D_MODEL), lambda b, t: (0, 0)),
            _resident((D_MODEL, 2 * D_FF), lambda b, t: (0, 0)),
            _resident((D_FF, D_MODEL), lambda b, t: (0, 0)),
        ],
        out_specs=pl.BlockSpec((None, TM, D_MODEL), lambda b, t: (b, t, 0)),
        out_shape=jax.ShapeDtypeStruct((bsz, n_tiles * TM, D_MODEL), F32),
        compiler_params=_cparams(("arbitrary", "arbitrary")),
        name="ffn",
    )(xa, modsel, nw, w_in, w_out)


def _inproj_kernel(x_ref, mod_ref, nw_ref, w_ref, cr_ref, sr_ref, cd_ref, sa_ref, sb_ref,
                   qr_ref, kr_ref, vr_ref, rg_ref, qd_ref, kd_ref, vd_ref, gt_ref):
    x = x_ref[...]
    h = _rms(x, nw_ref[2:3, :]) * (1.0 + mod_ref[4:5, :]) + mod_ref[3:4, :]
    hb = h.astype(BF16)
    cr, sr = cr_ref[...], sr_ref[...]
    cd, sa, sb = cd_ref[...], sa_ref[...], sb_ref[...]

    rq = _mm(hb, w_ref[:, OFF_RQ:OFF_RK])
    rk = _mm(hb, w_ref[:, OFF_RK:OFF_RV])
    for hd in range(H_RET):
        sl = slice(hd * RET_DK, (hd + 1) * RET_DK)
        q = rq[:, sl]
        k = rk[:, sl]
        qr_ref[:, sl] = (q * cr + pltpu.roll(q, 64, 1) * sr).astype(BF16)
        kr_ref[:, sl] = ((k * cr + pltpu.roll(k, 64, 1) * sr) * (RET_DK ** -0.5)).astype(BF16)
    vr_ref[...] = _mm(hb, w_ref[:, OFF_RV:OFF_RG]).astype(BF16)
    rg_ref[...] = _mm(hb, w_ref[:, OFF_RG:OFF_DQ]).astype(BF16)

    dq = _mm(hb, w_ref[:, OFF_DQ:OFF_DK])
    dk = _mm(hb, w_ref[:, OFF_DK:OFF_DV])
    dv = _mm(hb, w_ref[:, OFF_DV:OFF_GT])
    for hd in range(H_DIFF):
        sl = slice(hd * DIFF_DV, (hd + 1) * DIFF_DV)
        q = dq[:, sl]
        k = dk[:, sl]
        qrot = q * cd + pltpu.roll(q, 16, 1) * sa + pltpu.roll(q, 112, 1) * sb
        krot = k * cd + pltpu.roll(k, 16, 1) * sa + pltpu.roll(k, 112, 1) * sb
        qd_ref[hd] = (qrot * (DIFF_DH ** -0.5)).astype(BF16)
        kd_ref[hd] = krot.astype(BF16)
        vd_ref[hd] = dv[:, sl].astype(BF16)
    gt_ref[...] = _mm(hb, w_ref[:, OFF_GT:IN_WIDTH]).astype(BF16)


def _inproj(xa, modsel, nw, w_in, tables, *, ctx_tile):
    bsz, n, _ = xa.shape
    n_tiles = n // TM
    tok = lambda width: pl.BlockSpec((None, TM, width), lambda b, t: (b, t, 0))
    tab = pl.BlockSpec((TM, 128), lambda b, t: (t, 0))
    heads = pl.BlockSpec((None, H_DIFF, TM, DIFF_DV), lambda b, t: (b, 0, t, 0))
    sds = jax.ShapeDtypeStruct
    return pl.pallas_call(
        _inproj_kernel,
        grid=(bsz, n_tiles),
        in_specs=[
            tok(D_MODEL),
            pl.BlockSpec((None, None, N_MOD, D_MODEL), lambda b, t: (b, t // ctx_tile, 0, 0)),
            _resident((6, D_MODEL), lambda b, t: (0, 0)),
            _resident((D_MODEL, IN_WIDTH), lambda b, t: (0, 0)),
            tab, tab, tab, tab, tab,
        ],
        out_specs=[tok(512), tok(512), tok(1024), tok(1024), heads, heads, heads, tok(2048)],
        out_shape=[
            sds((bsz, n, 512), BF16), sds((bsz, n, 512), BF16),
            sds((bsz, n, 1024), BF16), sds((bsz, n, 1024), BF16),
            sds((bsz, H_DIFF, n, DIFF_DV), BF16), sds((bsz, H_DIFF, n, DIFF_DV), BF16),
            sds((bsz, H_DIFF, n, DIFF_DV), BF16), sds((bsz, n, 2048), BF16),
        ],
        compiler_params=_cparams(("arbitrary", "arbitrary")),
        name="inproj",
    )(xa, modsel, nw, w_in, *tables)


def _retention_kernel(dl_ref, qf_ref, kf_ref, vf_ref, qb_ref, kb_ref, vb_ref, yf_ref, yb_ref,
                      st_ref, dm_ref, zt_ref, xi_ref):
    c = RET_BLOCK
    dl = dl_ref[...]
    lg = jnp.minimum(dl, 0.0) - jnp.log(1.0 + jnp.exp(-jnp.abs(dl)))

    @pl.when(pl.program_id(1) == 0)
    def _init():
        st_ref[...] = jnp.zeros_like(st_ref)
        ri = lax.broadcasted_iota(jnp.int32, (c, c), 0).astype(F32)
        ci = lax.broadcasted_iota(jnp.int32, (c, c), 1).astype(F32)
        pos = lax.broadcasted_iota(jnp.int32, (c, RET_DK), 0).astype(F32)
        for d in range(2):
            diff = ri - ci if d == 0 else ci - ri
            for hd in range(H_RET):
                g = lg[d:d + 1, hd:hd + 1]
                dm_ref[d, hd] = jnp.where(diff >= 0.0, jnp.exp(jnp.maximum(diff, 0.0) * g), 0.0)
                if d == 0:
                    zt_ref[d, hd] = jnp.exp((c - 1.0 - pos) * g)
                    xi_ref[d, hd] = jnp.exp((pos + 1.0) * g)
                else:
                    zt_ref[d, hd] = jnp.exp(pos * g)
                    xi_ref[d, hd] = jnp.exp((c - pos) * g)

    for d, (q_ref, k_ref, v_ref, y_ref) in enumerate(
            ((qf_ref, kf_ref, vf_ref, yf_ref), (qb_ref, kb_ref, vb_ref, yb_ref))):
        for hd in range(H_RET):
            g_blk = jnp.exp(float(c) * lg[d:d + 1, hd:hd + 1])
            q = q_ref[:, hd * RET_DK:(hd + 1) * RET_DK]
            k = k_ref[:, hd * RET_DK:(hd + 1) * RET_DK]
            v = v_ref[:, hd * RET_DV:(hd + 1) * RET_DV]
            s = _mm_nt(q, k) * dm_ref[d, hd]
            inner = _mm(s.astype(BF16), v)
            state = st_ref[d, hd]
            qx = (q.astype(F32) * xi_ref[d, hd]).astype(BF16)
            cross = _mm(qx, state.astype(BF16))
            y_ref[:, hd * RET_DV:(hd + 1) * RET_DV] = (inner + cross).astype(BF16)
            kz = (k.astype(F32) * zt_ref[d, hd]).astype(BF16)
            st_ref[d, hd] = g_blk * state + _mm_tn(kz, v)


def _retention(decay_logit, q, k, v):
    bsz, n, _ = q.shape
    nb = n // RET_BLOCK
    last = nb - 1
    fwd = lambda b, i: (b, jnp.where(i == 0, last, i - 1), 0)
    bwd = lambda b, i: (b, jnp.where(i == 0, last, last - i), 0)
    blk = lambda width, im: pl.BlockSpec((None, RET_BLOCK, width), im)
    y = jax.ShapeDtypeStruct((bsz, n, H_RET * RET_DV), BF16)
    return pl.pallas_call(
        _retention_kernel,
        grid=(bsz, nb),
        in_specs=[
            pl.BlockSpec((2, H_RET), lambda b, i: (0, 0)),
            blk(512, fwd), blk(512, fwd), blk(1024, fwd),
            blk(512, bwd), blk(512, bwd), blk(1024, bwd),
        ],
        out_specs=[blk(1024, fwd), blk(1024, bwd)],
        out_shape=[y, y],
        scratch_shapes=[
            pltpu.VMEM((2, H_RET, RET_DK, RET_DV), F32),
            pltpu.VMEM((2, H_RET, RET_BLOCK, RET_BLOCK), F32),
            pltpu.VMEM((2, H_RET, RET_BLOCK, RET_DK), F32),
            pltpu.VMEM((2, H_RET, RET_BLOCK, RET_DK), F32),
        ],
        compiler_params=_cparams(("arbitrary", "arbitrary")),
        name="retention",
    )(decay_logit, q, k, v, q, k, v)


def _attn_kernel(lam_ref, sw_ref, q_ref, k_ref, v_ref, o_ref, qq_ref, m_ref, l_ref, acc_ref,
                 *, tq, tk, n_chunks, lam_init):
    qt = q_ref[...].astype(F32).T
    row = lax.broadcasted_iota(jnp.int32, qt.shape, 0)
    qq_ref[:, :tq] = jnp.where(row < DIFF_DH, qt, 0.0).astype(BF16)
    qq_ref[:, tq:] = jnp.where(row >= DIFF_DH, qt, 0.0).astype(BF16)
    m_ref[...] = jnp.full_like(m_ref, -jnp.inf)
    l_ref[...] = jnp.zeros_like(l_ref)
    acc_ref[...] = jnp.zeros_like(acc_ref)

    def body(ci, carry):
        start = pl.multiple_of(ci * tk, tk)
        kc = k_ref[pl.ds(start, tk), :]
        vc = v_ref[pl.ds(start, tk), :]
        s = _mm(kc, qq_ref[...])
        m_old = m_ref[...]
        m_new = jnp.maximum(m_old, jnp.max(s, axis=0, keepdims=True))
        alpha = jnp.exp(m_old - m_new)
        p = jnp.exp(s - m_new)
        l_ref[...] = alpha * l_ref[...] + jnp.sum(p, axis=0, keepdims=True)
        acc_ref[...] = alpha * acc_ref[...] + _mm_tn(vc, p.astype(BF16))
        m_ref[...] = m_new
        return carry

    lax.fori_loop(0, n_chunks, body, 0)

    lv = lam_ref[...]
    lam = (jnp.exp(jnp.sum(lv[0:1, :] * lv[1:2, :], axis=1, keepdims=True))
           - jnp.exp(jnp.sum(lv[2:3, :] * lv[3:4, :], axis=1, keepdims=True)) + lam_init)
    acc = acc_ref[...]
    l = l_ref[...]
    o = acc[:, :tq] / l[:, :tq] - lam * (acc[:, tq:] / l[:, tq:])
    o = o * lax.rsqrt(jnp.mean(o * o, axis=0, keepdims=True) + EPS) * sw_ref[...] * (1.0 - lam_init)
    o_ref[...] = o.T.astype(BF16)


def _attention(lam_vec, subln_w, q, k, v, *, tq, tk, q_block0, n_q, kv_block0, n_kv, lam_init):
    bsz = q.shape[0]
    n_chunks = n_kv // tk
    kern = functools.partial(_attn_kernel, tq=tq, tk=tk, n_chunks=n_chunks, lam_init=lam_init)
    return pl.pallas_call(
        kern,
        grid=(bsz, H_DIFF, n_q),
        in_specs=[
            pl.BlockSpec((4, DIFF_DH), lambda b, h, i: (0, 0)),
            pl.BlockSpec((DIFF_DV, 1), lambda b, h, i: (0, 0)),
            pl.BlockSpec((None, None, tq, DIFF_DV), lambda b, h, i: (b, h, q_block0 + i, 0)),
            pl.BlockSpec((None, None, n_kv, DIFF_DV), lambda b, h, i: (b, h, kv_block0, 0)),
            pl.BlockSpec((None, None, n_kv, DIFF_DV), lambda b, h, i: (b, h, kv_block0, 0)),
        ],
        out_specs=pl.BlockSpec((None, tq, DIFF_DV), lambda b, h, i: (b, i, h)),
        out_shape=jax.ShapeDtypeStruct((bsz, n_q * tq, H_DIFF * DIFF_DV), BF16),
        scratch_shapes=[
            pltpu.VMEM((DIFF_DV, 2 * tq), BF16),
            pltpu.VMEM((1, 2 * tq), F32),
            pltpu.VMEM((1, 2 * tq), F32),
            pltpu.VMEM((DIFF_DV, 2 * tq), F32),
        ],
        compiler_params=_cparams(("arbitrary", "arbitrary", "arbitrary")),
        name="diffattn",
    )(lam_vec, subln_w.reshape(DIFF_DV, 1), q, k, v)


def _merge_kernel(x_ref, mod_ref, nw_ref, gn_ref, yf_ref, yb_ref, rg_ref, al_ref, ac_ref, gt_ref, w_ref, o_ref,
                  *, ctx_tile):
    x = x_ref[...]
    is_ctx = pl.program_id(1) == ctx_tile
    diff = jnp.where(is_ctx, ac_ref[...], al_ref[...]).astype(F32)
    y = yf_ref[...].astype(F32) + yb_ref[...].astype(F32)
    rg = rg_ref[...].astype(F32)
    parts = []
    for hd in range(H_RET):
        sl = slice(hd * RET_DV, (hd + 1) * RET_DV)
        yh = y[:, sl]
        mu = jnp.mean(yh, axis=-1, keepdims=True)
        yc = yh - mu
        var = jnp.mean(yc * yc, axis=-1, keepdims=True)
        parts.append(yc * lax.rsqrt(var + EPS) * gn_ref[:, sl])
    ret = jnp.concatenate(parts, axis=1) * (rg * jax.nn.sigmoid(rg))
    ga = jax.nn.sigmoid(gt_ref[:, :D_MODEL].astype(F32))
    gb = jax.nn.sigmoid(gt_ref[:, D_MODEL:].astype(F32))
    z = (ga * ret + gb * diff).astype(BF16)
    yo = _mm(z, w_ref[...])
    o_ref[...] = x + mod_ref[5:6, :] * _rms(yo, nw_ref[3:4, :])


def _merge(xa, modsel, nw, gn_w, yf, yb, rg, at_lat, at_ctx, gates, w_out, *, ctx_tile):
    bsz, n, _ = xa.shape
    n_tiles = n // TM
    tok = lambda width: pl.BlockSpec((None, TM, width), lambda b, t: (b, t, 0))
    kern = functools.partial(_merge_kernel, ctx_tile=ctx_tile)
    return pl.pallas_call(
        kern,
        grid=(bsz, n_tiles),
        in_specs=[
            tok(D_MODEL),
            pl.BlockSpec((None, None, N_MOD, D_MODEL), lambda b, t: (b, t // ctx_tile, 0, 0)),
            _resident((6, D_MODEL), lambda b, t: (0, 0)),
            _resident((1, D_MODEL), lambda b, t: (0, 0)),
            tok(1024), tok(1024), tok(1024),
            pl.BlockSpec((None, TM, D_MODEL), lambda b, t: (b, jnp.minimum(t, ctx_tile - 1), 0)),
            pl.BlockSpec((None, TM, D_MODEL), lambda b, t: (b, 0, 0)),
            tok(2048),
            _resident((D_MODEL, D_MODEL), lambda b, t: (0, 0)),
        ],
        out_specs=tok(D_MODEL),
        out_shape=jax.ShapeDtypeStruct((bsz, n, D_MODEL), F32),
        compiler_params=_cparams(("arbitrary", "arbitrary")),
        name="merge",
    )(xa, modsel, nw, gn_w, yf, yb, rg, at_lat, at_ctx, gates, w_out)


def _rope_tables(t, n):
    pos = jnp.arange(n, dtype=jnp.int32)
    is_lat = (pos < t)[:, None]
    lane = jnp.arange(128)
    inv_r = ROPE_BASE ** (-jnp.arange(64, dtype=F32) / 64)
    ang = pos.astype(F32)[:, None] * inv_r[None, :]
    cos, sin = jnp.cos(ang), jnp.sin(ang)
    cr = jnp.where(is_lat, jnp.concatenate([cos, cos], axis=1), 1.0)
    sr = jnp.where(is_lat, jnp.concatenate([-sin, sin], axis=1), 0.0)
    inv_d = ROPE_BASE ** (-jnp.arange(16, dtype=F32) / 16)
    row = (pos // GRID_W).astype(F32)[:, None] * inv_d[None, :]
    col = (pos % GRID_W).astype(F32)[:, None] * inv_d[None, :]
    cos64 = jnp.concatenate([jnp.cos(row)] * 2 + [jnp.cos(col)] * 2, axis=1)
    sin64 = jnp.concatenate([jnp.sin(row)] * 2 + [jnp.sin(col)] * 2, axis=1)
    cos128 = jnp.concatenate([cos64, cos64], axis=1)
    sin128 = jnp.concatenate([sin64, sin64], axis=1)
    upper = ((lane % 32) >= 16)[None, :]
    cd = jnp.where(is_lat, cos128, 1.0)
    sa = jnp.where(is_lat & upper, sin128, 0.0)
    sb = jnp.where(is_lat & ~upper, -sin128, 0.0)
    return cr, sr, cd, sa, sb


def kernel(x, c, ctx, c_ctx, w_ada, b_ada, norm_w, ffn_w_in, ffn_w_out, w_in, w_out,
           ret_decay_logit, ret_gn_w, diff_lambda, diff_subln_w):
    bsz, t, d = x.shape
    n_ctx = ctx.shape[1]
    depth = w_ada.shape[0]
    assert d == D_MODEL and n_ctx == TM and t % ATT_TQ == 0 and bsz + 1 <= 8
    n = t + n_ctx
    assert n % ATT_TK == 0
    n_tiles = n // TM
    ctx_tile = n_tiles - 1

    xa = jnp.concatenate([x, ctx], axis=1)
    cs = jnp.zeros((8, d), F32).at[:bsz].set(c).at[bsz].set(c_ctx)
    mod = _adaln(cs, w_ada, b_ada).reshape(depth, 8, N_MOD, d)
    modsel = jnp.stack([mod[:, :bsz], jnp.broadcast_to(mod[:, bsz:bsz + 1], (depth, bsz, N_MOD, d))], axis=2)
    tables = _rope_tables(t, n)

    ffn_w_in_b = ffn_w_in.astype(BF16)
    ffn_w_out_b = ffn_w_out.astype(BF16)
    w_in_b = w_in.astype(BF16)
    w_out_b = w_out.astype(BF16)

    out = None
    for l in range(depth):
        last = l == depth - 1
        lam_init = 0.8 - 0.6 * math.exp(-0.3 * l)
        nw = norm_w[l]
        xa = _ffn(xa, modsel[l], nw, ffn_w_in_b[l, 0], ffn_w_out_b[l, 0],
                  mod_off=0, nw_off=0, n_tiles=n_tiles, ctx_tile=ctx_tile)
        qr, kr, vr, rg, qd, kd, vd, gates = _inproj(xa, modsel[l], nw, w_in_b[l], tables, ctx_tile=ctx_tile)
        yf, yb = _retention(ret_decay_logit[l], qr, kr, vr)
        at_lat = _attention(diff_lambda[l], diff_subln_w[l], qd, kd, vd, tq=ATT_TQ, tk=ATT_TK,
                            q_block0=0, n_q=t // ATT_TQ, kv_block0=0, n_kv=n, lam_init=lam_init)
        at_ctx = _attention(diff_lambda[l], diff_subln_w[l], qd, kd, vd, tq=TM, tk=TM,
                            q_block0=ctx_tile, n_q=1, kv_block0=ctx_tile, n_kv=TM, lam_init=lam_init)
        xa = _merge(xa, modsel[l], nw, ret_gn_w[l].reshape(1, d), yf, yb, rg, at_lat, at_ctx, gates,
                    w_out_b[l], ctx_tile=ctx_tile)
        if last:
            out = _ffn(xa, modsel[l], nw, ffn_w_in_b[l, 1], ffn_w_out_b[l, 1],
                       mod_off=6, nw_off=4, n_tiles=t // TM, ctx_tile=ctx_tile)
        else:
            xa = _ffn(xa, modsel[l], nw, ffn_w_in_b[l, 1], ffn_w_out_b[l, 1],
                      mod_off=6, nw_off=4, n_tiles=n_tiles, ctx_tile=ctx_tile)
    return out
```

```python
import functools
import math

import jax
import jax.numpy as jnp
from jax import lax
from jax.experimental import pallas as pl
from jax.experimental.pallas import tpu as pltpu

D_MODEL = 1024
D_FF = 2816
H_RET = 4
RET_DK = 128
RET_DV = 256
H_DIFF = 8
DIFF_DV = 128
DIFF_DH = 64
GRID_W = 64
ROPE_BASE = 10000.0
EPS = 1e-6
LOG2E = 1.4426950408889634
N_MOD = 9
IN_WIDTH = 8192
OFF_RQ, OFF_RK, OFF_RV, OFF_RG, OFF_DQ, OFF_DK, OFF_DV, OFF_GT = 0, 512, 1024, 2048, 3072, 4096, 5120, 6144

TM = 256
RET_BLOCK = 256
ATT_TQ = 512
ATT_TK = 640
VMEM_LIMIT = 56 * 1024 * 1024

BF16 = jnp.bfloat16
F32 = jnp.float32


def _cparams(sem):
    return pltpu.CompilerParams(dimension_semantics=sem, vmem_limit_bytes=VMEM_LIMIT)


def _resident(shape, index_map):
    return pl.BlockSpec(shape, index_map, pipeline_mode=pl.Buffered(1))


def _rms(x, g):
    return x * lax.rsqrt(jnp.mean(x * x, axis=-1, keepdims=True) + EPS) * g


def _mm(a, b):
    return jnp.dot(a, b, preferred_element_type=F32)


def _mm_nt(a, b):
    return lax.dot_general(a, b, (((1,), (1,)), ((), ())), preferred_element_type=F32)


def _mm_tn(a, b):
    return lax.dot_general(a, b, (((0,), (0,)), ((), ())), preferred_element_type=F32)


def _adaln_kernel(c_ref, w_ref, b_ref, o_ref):
    c = c_ref[...]
    s = c * jax.nn.sigmoid(c)
    o_ref[...] = _mm(s, w_ref[...]) + b_ref[...]


def _adaln(cs, w_ada, b_ada):
    depth = w_ada.shape[0]
    width = w_ada.shape[2]
    tn = 1024
    return pl.pallas_call(
        _adaln_kernel,
        grid=(depth, width // tn),
        in_specs=[
            pl.BlockSpec((8, D_MODEL), lambda l, j: (0, 0)),
            pl.BlockSpec((None, D_MODEL, tn), lambda l, j: (l, 0, j)),
            pl.BlockSpec((None, 1, tn), lambda l, j: (l, 0, j)),
        ],
        out_specs=pl.BlockSpec((None, 8, tn), lambda l, j: (l, 0, j)),
        out_shape=jax.ShapeDtypeStruct((depth, 8, width), F32),
        compiler_params=_cparams(("arbitrary", "arbitrary")),
        name="adaln",
    )(cs, w_ada, b_ada.reshape(depth, 1, width))


def _ffn_kernel(x_ref, mod_ref, nw_ref, win_ref, wout_ref, o_ref, *, mod_off, nw_off):
    x = x_ref[...]
    shift = mod_ref[mod_off:mod_off + 1, :]
    scale = mod_ref[mod_off + 1:mod_off + 2, :]
    gate = mod_ref[mod_off + 2:mod_off + 3, :]
    h = _rms(x, nw_ref[nw_off:nw_off + 1, :]) * (1.0 + scale) + shift
    hb = h.astype(BF16)
    a = _mm(hb, win_ref[:, :D_FF])
    b = _mm(hb, win_ref[:, D_FF:])
    u = (a * jax.nn.sigmoid(a) * b).astype(BF16)
    y = _mm(u, wout_ref[...])
    o_ref[...] = x + 0.5 * gate * _rms(y, nw_ref[nw_off + 1:nw_off + 2, :])


def _ffn(xa, modsel, nw, w_in, w_out, *, mod_off, nw_off, n_tiles, ctx_tile):
    bsz = xa.shape[0]
    kern = functools.partial(_ffn_kernel, mod_off=mod_off, nw_off=nw_off)
    return pl.pallas_call(
        kern,
        grid=(bsz, n_tiles),
        in_specs=[
            pl.BlockSpec((None, TM, D_MODEL), lambda b, t: (b, t, 0)),
            pl.BlockSpec((None, None, N_MOD, D_MODEL), lambda b, t: (b, t // ctx_tile, 0, 0)),
            _resident((6, D_MODEL), lambda b, t: (0, 0)),
            _resident((D_MODEL, 2 * D_FF), lambda b, t: (0, 0)),
            _resident((D_FF, D_MODEL), lambda b, t: (0, 0)),
        ],
        out_specs=pl.BlockSpec((None, TM, D_MODEL), lambda b, t: (b, t, 0)),
        out_shape=jax.ShapeDtypeStruct((bsz, n_tiles * TM, D_MODEL), F32),
        compiler_params=_cparams(("arbitrary", "arbitrary")),
        name="ffn",
    )(xa, modsel, nw, w_in, w_out)


def _inproj_kernel(x_ref, mod_ref, nw_ref, w_ref, cr_ref, sr_ref, cd_ref, sa_ref, sb_ref,
                   qr_ref, kr_ref, vr_ref, rg_ref, qd_ref, kd_ref, vd_ref, gt_ref):
    x = x_ref[...]
    h = _rms(x, nw_ref[2:3, :]) * (1.0 + mod_ref[4:5, :]) + mod_ref[3:4, :]
    hb = h.astype(BF16)
    cr, sr = cr_ref[...], sr_ref[...]
    cd, sa, sb = cd_ref[...], sa_ref[...], sb_ref[...]

    rq = _mm(hb, w_ref[:, OFF_RQ:OFF_RK])
    rk = _mm(hb, w_ref[:, OFF_RK:OFF_RV])
    for hd in range(H_RET):
        sl = slice(hd * RET_DK, (hd + 1) * RET_DK)
        q = rq[:, sl]
        k = rk[:, sl]
        qr_ref[:, sl] = (q * cr + pltpu.roll(q, 64, 1) * sr).astype(BF16)
        kr_ref[:, sl] = ((k * cr + pltpu.roll(k, 64, 1) * sr) * (RET_DK ** -0.5)).astype(BF16)
    vr_ref[...] = _mm(hb, w_ref[:, OFF_RV:OFF_RG]).astype(BF16)
    rg_ref[...] = _mm(hb, w_ref[:, OFF_RG:OFF_DQ]).astype(BF16)

    dq = _mm(hb, w_ref[:, OFF_DQ:OFF_DK])
    dk = _mm(hb, w_ref[:, OFF_DK:OFF_DV])
    dv = _mm(hb, w_ref[:, OFF_DV:OFF_GT])
    for hd in range(H_DIFF):
        sl = slice(hd * DIFF_DV, (hd + 1) * DIFF_DV)
        q = dq[:, sl]
        k = dk[:, sl]
        qrot = q * cd + pltpu.roll(q, 16, 1) * sa + pltpu.roll(q, 112, 1) * sb
        krot = k * cd + pltpu.roll(k, 16, 1) * sa + pltpu.roll(k, 112, 1) * sb
        qd_ref[hd] = (qrot * (DIFF_DH ** -0.5 * LOG2E)).astype(BF16)
        kd_ref[hd] = krot.astype(BF16)
        vd_ref[hd] = dv[:, sl].astype(BF16)
    gt_ref[...] = _mm(hb, w_ref[:, OFF_GT:IN_WIDTH]).astype(BF16)


def _inproj(xa, modsel, nw, w_in, tables, *, ctx_tile):
    bsz, n, _ = xa.shape
    n_tiles = n // TM
    tok = lambda width: pl.BlockSpec((None, TM, width), lambda b, t: (b, t, 0))
    tab = pl.BlockSpec((TM, 128), lambda b, t: (t, 0))
    heads = pl.BlockSpec((None, H_DIFF, TM, DIFF_DV), lambda b, t: (b, 0, t, 0))
    sds = jax.ShapeDtypeStruct
    return pl.pallas_call(
        _inproj_kernel,
        grid=(bsz, n_tiles),
        in_specs=[
            tok(D_MODEL),
            pl.BlockSpec((None, None, N_MOD, D_MODEL), lambda b, t: (b, t // ctx_tile, 0, 0)),
            _resident((6, D_MODEL), lambda b, t: (0, 0)),
            _resident((D_MODEL, IN_WIDTH), lambda b, t: (0, 0)),
            tab, tab, tab, tab, tab,
        ],
        out_specs=[tok(512), tok(512), tok(1024), tok(1024), heads, heads, heads, tok(2048)],
        out_shape=[
            sds((bsz, n, 512), BF16), sds((bsz, n, 512), BF16),
            sds((bsz, n, 1024), BF16), sds((bsz, n, 1024), BF16),
            sds((bsz, H_DIFF, n, DIFF_DV), BF16), sds((bsz, H_DIFF, n, DIFF_DV), BF16),
            sds((bsz, H_DIFF, n, DIFF_DV), BF16), sds((bsz, n, 2048), BF16),
        ],
        compiler_params=_cparams(("arbitrary", "arbitrary")),
        name="inproj",
    )(xa, modsel, nw, w_in, *tables)


def _retention_kernel(dl_ref, qf_ref, kf_ref, vf_ref, qb_ref, kb_ref, vb_ref, yf_ref, yb_ref,
                      st_ref, dm_ref, zt_ref, xi_ref):
    c = RET_BLOCK
    dl = dl_ref[...]
    lg = jnp.minimum(dl, 0.0) - jnp.log(1.0 + jnp.exp(-jnp.abs(dl)))

    @pl.when(pl.program_id(1) == 0)
    def _init():
        st_ref[...] = jnp.zeros_like(st_ref)
        ri = lax.broadcasted_iota(jnp.int32, (c, c), 0).astype(F32)
        ci = lax.broadcasted_iota(jnp.int32, (c, c), 1).astype(F32)
        pos = lax.broadcasted_iota(jnp.int32, (c, RET_DK), 0).astype(F32)
        for d in range(2):
            diff = ri - ci if d == 0 else ci - ri
            for hd in range(H_RET):
                g = lg[d:d + 1, hd:hd + 1]
                dm_ref[d, hd] = jnp.where(diff >= 0.0, jnp.exp(jnp.maximum(diff, 0.0) * g), 0.0)
                if d == 0:
                    zt_ref[d, hd] = jnp.exp((c - 1.0 - pos) * g)
                    xi_ref[d, hd] = jnp.exp((pos + 1.0) * g)
                else:
                    zt_ref[d, hd] = jnp.exp(pos * g)
                    xi_ref[d, hd] = jnp.exp((c - pos) * g)

    for d, (q_ref, k_ref, v_ref, y_ref) in enumerate(
            ((qf_ref, kf_ref, vf_ref, yf_ref), (qb_ref, kb_ref, vb_ref, yb_ref))):
        for hd in range(H_RET):
            g_blk = jnp.exp(float(c) * lg[d:d + 1, hd:hd + 1])
            q = q_ref[:, hd * RET_DK:(hd + 1) * RET_DK]
            k = k_ref[:, hd * RET_DK:(hd + 1) * RET_DK]
            v = v_ref[:, hd * RET_DV:(hd + 1) * RET_DV]
            s = _mm_nt(q, k) * dm_ref[d, hd]
            inner = _mm(s.astype(BF16), v)
            state = st_ref[d, hd]
            qx = (q.astype(F32) * xi_ref[d, hd]).astype(BF16)
            cross = _mm(qx, state.astype(BF16))
            y_ref[:, hd * RET_DV:(hd + 1) * RET_DV] = (inner + cross).astype(BF16)
            kz = (k.astype(F32) * zt_ref[d, hd]).astype(BF16)
            st_ref[d, hd] = g_blk * state + _mm_tn(kz, v)


def _retention(decay_logit, q, k, v):
    bsz, n, _ = q.shape
    nb = n // RET_BLOCK
    last = nb - 1
    fwd = lambda b, i: (b, jnp.where(i == 0, last, i - 1), 0)
    bwd = lambda b, i: (b, jnp.where(i == 0, last, last - i), 0)
    blk = lambda width, im: pl.BlockSpec((None, RET_BLOCK, width), im)
    y = jax.ShapeDtypeStruct((bsz, n, H_RET * RET_DV), BF16)
    return pl.pallas_call(
        _retention_kernel,
        grid=(bsz, nb),
        in_specs=[
            pl.BlockSpec((2, H_RET), lambda b, i: (0, 0)),
            blk(512, fwd), blk(512, fwd), blk(1024, fwd),
            blk(512, bwd), blk(512, bwd), blk(1024, bwd),
        ],
        out_specs=[blk(1024, fwd), blk(1024, bwd)],
        out_shape=[y, y],
        scratch_shapes=[
            pltpu.VMEM((2, H_RET, RET_DK, RET_DV), F32),
            pltpu.VMEM((2, H_RET, RET_BLOCK, RET_BLOCK), F32),
            pltpu.VMEM((2, H_RET, RET_BLOCK, RET_DK), F32),
            pltpu.VMEM((2, H_RET, RET_BLOCK, RET_DK), F32),
        ],
        compiler_params=_cparams(("arbitrary", "arbitrary")),
        name="retention",
    )(decay_logit, q, k, v, q, k, v)


def _attn_kernel(lam_ref, sw_ref, q_ref, k_ref, v_ref, o_ref, qq_ref, m_ref, l_ref, acc_ref, s_ref,
                 *, tq, tk, n_chunks, lam_init):
    qt = q_ref[...].astype(F32).T
    row = lax.broadcasted_iota(jnp.int32, qt.shape, 0)
    qq_ref[:, :tq] = jnp.where(row < DIFF_DH, qt, 0.0).astype(BF16)
    qq_ref[:, tq:] = jnp.where(row >= DIFF_DH, qt, 0.0).astype(BF16)
    m_ref[...] = jnp.full_like(m_ref, -jnp.inf)
    l_ref[...] = jnp.zeros_like(l_ref)
    acc_ref[...] = jnp.zeros_like(acc_ref)

    def scores(ci, slot):
        start = pl.multiple_of(ci * tk, tk)
        s_ref[slot] = _mm(k_ref[pl.ds(start, tk), :], qq_ref[...])

    def softmax_pv(ci, slot):
        start = pl.multiple_of(ci * tk, tk)
        vc = v_ref[pl.ds(start, tk), :]
        s = s_ref[slot]
        m_old = m_ref[...]
        m_new = jnp.maximum(m_old, jnp.max(s, axis=0, keepdims=True))
        alpha = jnp.exp2(m_old - m_new)
        p = jnp.exp2(s - m_new)
        l_ref[...] = alpha * l_ref[...] + jnp.sum(p, axis=0, keepdims=True)
        acc_ref[...] = alpha * acc_ref[...] + _mm_tn(vc, p.astype(BF16))
        m_ref[...] = m_new

    scores(0, 0)
    n_pairs = (n_chunks - 1) // 2

    def body(j, carry):
        c0 = 2 * j
        scores(c0 + 1, 1)
        softmax_pv(c0, 0)
        scores(c0 + 2, 0)
        softmax_pv(c0 + 1, 1)
        return carry

    if n_pairs > 0:
        lax.fori_loop(0, n_pairs, body, 0)
    if n_chunks - 2 * n_pairs == 2:
        scores(n_chunks - 1, 1)
        softmax_pv(n_chunks - 2, 0)
        softmax_pv(n_chunks - 1, 1)
    else:
        softmax_pv(n_chunks - 1, 0)

    lv = lam_ref[...]
    lam = (jnp.exp(jnp.sum(lv[0:1, :] * lv[1:2, :], axis=1, keepdims=True))
           - jnp.exp(jnp.sum(lv[2:3, :] * lv[3:4, :], axis=1, keepdims=True)) + lam_init)
    acc = acc_ref[...]
    l = l_ref[...]
    o = acc[:, :tq] / l[:, :tq] - lam * (acc[:, tq:] / l[:, tq:])
    o = o * lax.rsqrt(jnp.mean(o * o, axis=0, keepdims=True) + EPS) * sw_ref[...] * (1.0 - lam_init)
    o_ref[...] = o.T.astype(BF16)


def _attention(lam_vec, subln_w, q, k, v, *, tq, tk, q_block0, n_q, kv_block0, n_kv, lam_init):
    bsz = q.shape[0]
    n_chunks = n_kv // tk
    kern = functools.partial(_attn_kernel, tq=tq, tk=tk, n_chunks=n_chunks, lam_init=lam_init)
    return pl.pallas_call(
        kern,
        grid=(bsz, H_DIFF, n_q),
        in_specs=[
            pl.BlockSpec((4, DIFF_DH), lambda b, h, i: (0, 0)),
            pl.BlockSpec((DIFF_DV, 1), lambda b, h, i: (0, 0)),
            pl.BlockSpec((None, None, tq, DIFF_DV), lambda b, h, i: (b, h, q_block0 + i, 0)),
            pl.BlockSpec((None, None, n_kv, DIFF_DV), lambda b, h, i: (b, h, kv_block0, 0)),
            pl.BlockSpec((None, None, n_kv, DIFF_DV), lambda b, h, i: (b, h, kv_block0, 0)),
        ],
        out_specs=pl.BlockSpec((None, tq, DIFF_DV), lambda b, h, i: (b, i, h)),
        out_shape=jax.ShapeDtypeStruct((bsz, n_q * tq, H_DIFF * DIFF_DV), BF16),
        scratch_shapes=[
            pltpu.VMEM((DIFF_DV, 2 * tq), BF16),
            pltpu.VMEM((1, 2 * tq), F32),
            pltpu.VMEM((1, 2 * tq), F32),
            pltpu.VMEM((DIFF_DV, 2 * tq), F32),
            pltpu.VMEM((2, tk, 2 * tq), F32),
        ],
        compiler_params=_cparams(("arbitrary", "arbitrary", "arbitrary")),
        name="diffattn",
    )(lam_vec, subln_w.reshape(DIFF_DV, 1), q, k, v)


def _merge_kernel(x_ref, mod_ref, nw_ref, gn_ref, yf_ref, yb_ref, rg_ref, al_ref, ac_ref, gt_ref, w_ref, o_ref,
                  *, ctx_tile):
    x = x_ref[...]
    is_ctx = pl.program_id(1) == ctx_tile
    diff = jnp.where(is_ctx, ac_ref[...], al_ref[...]).astype(F32)
    y = yf_ref[...].astype(F32) + yb_ref[...].astype(F32)
    rg = rg_ref[...].astype(F32)
    parts = []
    for hd in range(H_RET):
        sl = slice(hd * RET_DV, (hd + 1) * RET_DV)
        yh = y[:, sl]
        mu = jnp.mean(yh, axis=-1, keepdims=True)
        yc = yh - mu
        var = jnp.mean(yc * yc, axis=-1, keepdims=True)
        parts.append(yc * lax.rsqrt(var + EPS) * gn_ref[:, sl])
    ret = jnp.concatenate(parts, axis=1) * (rg * jax.nn.sigmoid(rg))
    ga = jax.nn.sigmoid(gt_ref[:, :D_MODEL].astype(F32))
    gb = jax.nn.sigmoid(gt_ref[:, D_MODEL:].astype(F32))
    z = (ga * ret + gb * diff).astype(BF16)
    yo = _mm(z, w_ref[...])
    o_ref[...] = x + mod_ref[5:6, :] * _rms(yo, nw_ref[3:4, :])


def _merge(xa, modsel, nw, gn_w, yf, yb, rg, at_lat, at_ctx, gates, w_out, *, ctx_tile):
    bsz, n, _ = xa.shape
    n_tiles = n // TM
    tok = lambda width: pl.BlockSpec((None, TM, width), lambda b, t: (b, t, 0))
    kern = functools.partial(_merge_kernel, ctx_tile=ctx_tile)
    return pl.pallas_call(
        kern,
        grid=(bsz, n_tiles),
        in_specs=[
            tok(D_MODEL),
            pl.BlockSpec((None, None, N_MOD, D_MODEL), lambda b, t: (b, t // ctx_tile, 0, 0)),
            _resident((6, D_MODEL), lambda b, t: (0, 0)),
            _resident((1, D_MODEL), lambda b, t: (0, 0)),
            tok(1024), tok(1024), tok(1024),
            pl.BlockSpec((None, TM, D_MODEL), lambda b, t: (b, jnp.minimum(t, ctx_tile - 1), 0)),
            pl.BlockSpec((None, TM, D_MODEL), lambda b, t: (b, 0, 0)),
            tok(2048),
            _resident((D_MODEL, D_MODEL), lambda b, t: (0, 0)),
        ],
        out_specs=tok(D_MODEL),
        out_shape=jax.ShapeDtypeStruct((bsz, n, D_MODEL), F32),
        compiler_params=_cparams(("arbitrary", "arbitrary")),
        name="merge",
    )(xa, modsel, nw, gn_w, yf, yb, rg, at_lat, at_ctx, gates, w_out)


def _rope_tables(t, n):
    pos = jnp.arange(n, dtype=jnp.int32)
    is_lat = (pos < t)[:, None]
    lane = jnp.arange(128)
    inv_r = ROPE_BASE ** (-jnp.arange(64, dtype=F32) / 64)
    ang = pos.astype(F32)[:, None] * inv_r[None, :]
    cos, sin = jnp.cos(ang), jnp.sin(ang)
    cr = jnp.where(is_lat, jnp.concatenate([cos, cos], axis=1), 1.0)
    sr = jnp.where(is_lat, jnp.concatenate([-sin, sin], axis=1), 0.0)
    inv_d = ROPE_BASE ** (-jnp.arange(16, dtype=F32) / 16)
    row = (pos // GRID_W).astype(F32)[:, None] * inv_d[None, :]
    col = (pos % GRID_W).astype(F32)[:, None] * inv_d[None, :]
    cos64 = jnp.concatenate([jnp.cos(row)] * 2 + [jnp.cos(col)] * 2, axis=1)
    sin64 = jnp.concatenate([jnp.sin(row)] * 2 + [jnp.sin(col)] * 2, axis=1)
    cos128 = jnp.concatenate([cos64, cos64], axis=1)
    sin128 = jnp.concatenate([sin64, sin64], axis=1)
    upper = ((lane % 32) >= 16)[None, :]
    cd = jnp.where(is_lat, cos128, 1.0)
    sa = jnp.where(is_lat & upper, sin128, 0.0)
    sb = jnp.where(is_lat & ~upper, -sin128, 0.0)
    return cr, sr, cd, sa, sb


def kernel(x, c, ctx, c_ctx, w_ada, b_ada, norm_w, ffn_w_in, ffn_w_out, w_in, w_out,
           ret_decay_logit, ret_gn_w, diff_lambda, diff_subln_w):
    bsz, t, d = x.shape
    n_ctx = ctx.shape[1]
    depth = w_ada.shape[0]
    assert d == D_MODEL and n_ctx == TM and t % ATT_TQ == 0 and bsz + 1 <= 8
    n = t + n_ctx
    assert n % ATT_TK == 0
    n_tiles = n // TM
    ctx_tile = n_tiles - 1

    xa = jnp.concatenate([x, ctx], axis=1)
    cs = jnp.zeros((8, d), F32).at[:bsz].set(c).at[bsz].set(c_ctx)
    mod = _adaln(cs, w_ada, b_ada).reshape(depth, 8, N_MOD, d)
    modsel = jnp.stack([mod[:, :bsz], jnp.broadcast_to(mod[:, bsz:bsz + 1], (depth, bsz, N_MOD, d))], axis=2)
    tables = _rope_tables(t, n)

    ffn_w_in_b = ffn_w_in.astype(BF16)
    ffn_w_out_b = ffn_w_out.astype(BF16)
    w_in_b = w_in.astype(BF16)
    w_out_b = w_out.astype(BF16)

    out = None
    for l in range(depth):
        last = l == depth - 1
        lam_init = 0.8 - 0.6 * math.exp(-0.3 * l)
        nw = norm_w[l]
        xa = _ffn(xa, modsel[l], nw, ffn_w_in_b[l, 0], ffn_w_out_b[l, 0],
                  mod_off=0, nw_off=0, n_tiles=n_tiles, ctx_tile=ctx_tile)
        qr, kr, vr, rg, qd, kd, vd, gates = _inproj(xa, modsel[l], nw, w_in_b[l], tables, ctx_tile=ctx_tile)
        yf, yb = _retention(ret_decay_logit[l], qr, kr, vr)
        at_lat = _attention(diff_lambda[l], diff_subln_w[l], qd, kd, vd, tq=ATT_TQ, tk=ATT_TK,
                            q_block0=0, n_q=t // ATT_TQ, kv_block0=0, n_kv=n, lam_init=lam_init)
        at_ctx = _attention(diff_lambda[l], diff_subln_w[l], qd, kd, vd, tq=TM, tk=TM,
                            q_block0=ctx_tile, n_q=1, kv_block0=ctx_tile, n_kv=TM, lam_init=lam_init)
        xa = _merge(xa, modsel[l], nw, ret_gn_w[l].reshape(1, d), yf, yb, rg, at_lat, at_ctx, gates,
                    w_out_b[l], ctx_tile=ctx_tile)
        if last:
            out = _ffn(xa, modsel[l], nw, ffn_w_in_b[l, 1], ffn_w_out_b[l, 1],
                       mod_off=6, nw_off=4, n_tiles=t // TM, ctx_tile=ctx_tile)
        else:
            xa = _ffn(xa, modsel[l], nw, ffn_w_in_b[l, 1], ffn_w_out_b[l, 1],
                      mod_off=6, nw_off=4, n_tiles=n_tiles, ctx_tile=ctx_tile)
    return out
```

```python
import functools
import math

import jax
import jax.numpy as jnp
from jax import lax
from jax.experimental import pallas as pl
from jax.experimental.pallas import tpu as pltpu

D_MODEL = 1024
D_FF = 2816
H_RET = 4
RET_DK = 128
RET_DV = 256
H_DIFF = 8
DIFF_DV = 128
DIFF_DH = 64
GRID_W = 64
ROPE_BASE = 10000.0
EPS = 1e-6
LOG2E = 1.4426950408889634
N_MOD = 9
IN_WIDTH = 8192
OFF_RQ, OFF_RK, OFF_RV, OFF_RG, OFF_DQ, OFF_DK, OFF_DV, OFF_GT = 0, 512, 1024, 2048, 3072, 4096, 5120, 6144

TM = 256
RET_BLOCK = 256
ATT_TQ = 512
ATT_TK = 1280
ATT_UNROLL = 2
VMEM_LIMIT = 56 * 1024 * 1024

BF16 = jnp.bfloat16
F32 = jnp.float32


def _cparams(sem):
    return pltpu.CompilerParams(dimension_semantics=sem, vmem_limit_bytes=VMEM_LIMIT)


def _resident(shape, index_map):
    return pl.BlockSpec(shape, index_map, pipeline_mode=pl.Buffered(1))


def _rms(x, g):
    return x * lax.rsqrt(jnp.mean(x * x, axis=-1, keepdims=True) + EPS) * g


def _mm(a, b):
    return jnp.dot(a, b, preferred_element_type=F32)


def _mm_nt(a, b):
    return lax.dot_general(a, b, (((1,), (1,)), ((), ())), preferred_element_type=F32)


def _mm_tn(a, b):
    return lax.dot_general(a, b, (((0,), (0,)), ((), ())), preferred_element_type=F32)


def _adaln_kernel(c_ref, w_ref, b_ref, o_ref):
    c = c_ref[...]
    s = c * jax.nn.sigmoid(c)
    o_ref[...] = _mm(s, w_ref[...]) + b_ref[...]


def _adaln(cs, w_ada, b_ada):
    depth = w_ada.shape[0]
    width = w_ada.shape[2]
    tn = 1024
    return pl.pallas_call(
        _adaln_kernel,
        grid=(depth, width // tn),
        in_specs=[
            pl.BlockSpec((8, D_MODEL), lambda l, j: (0, 0)),
            pl.BlockSpec((None, D_MODEL, tn), lambda l, j: (l, 0, j)),
            pl.BlockSpec((None, 1, tn), lambda l, j: (l, 0, j)),
        ],
        out_specs=pl.BlockSpec((None, 8, tn), lambda l, j: (l, 0, j)),
        out_shape=jax.ShapeDtypeStruct((depth, 8, width), F32),
        compiler_params=_cparams(("arbitrary", "arbitrary")),
        name="adaln",
    )(cs, w_ada, b_ada.reshape(depth, 1, width))


def _ffn_kernel(x_ref, mod_ref, nw_ref, win_ref, wout_ref, o_ref, *, mod_off, nw_off):
    x = x_ref[...]
    shift = mod_ref[mod_off:mod_off + 1, :]
    scale = mod_ref[mod_off + 1:mod_off + 2, :]
    gate = mod_ref[mod_off + 2:mod_off + 3, :]
    h = _rms(x, nw_ref[nw_off:nw_off + 1, :]) * (1.0 + scale) + shift
    hb = h.astype(BF16)
    a = _mm(hb, win_ref[:, :D_FF])
    b = _mm(hb, win_ref[:, D_FF:])
    u = (a * jax.nn.sigmoid(a) * b).astype(BF16)
    y = _mm(u, wout_ref[...])
    o_ref[...] = x + 0.5 * gate * _rms(y, nw_ref[nw_off + 1:nw_off + 2, :])


def _ffn(xa, modsel, nw, w_in, w_out, *, mod_off, nw_off, n_tiles, ctx_tile):
    bsz = xa.shape[0]
    kern = functools.partial(_ffn_kernel, mod_off=mod_off, nw_off=nw_off)
    return pl.pallas_call(
        kern,
        grid=(bsz, n_tiles),
        in_specs=[
            pl.BlockSpec((None, TM, D_MODEL), lambda b, t: (b, t, 0)),
            pl.BlockSpec((None, None, N_MOD, D_MODEL), lambda b, t: (b, t // ctx_tile, 0, 0)),
            _resident((6, D_MODEL), lambda b, t: (0, 0)),
            _resident((D_MODEL, 2 * D_FF), lambda b, t: (0, 0)),
            _resident((D_FF, D_MODEL), lambda b, t: (0, 0)),
        ],
        out_specs=pl.BlockSpec((None, TM, D_MODEL), lambda b, t: (b, t, 0)),
        out_shape=jax.ShapeDtypeStruct((bsz, n_tiles * TM, D_MODEL), F32),
        compiler_params=_cparams(("arbitrary", "arbitrary")),
        name="ffn",
    )(xa, modsel, nw, w_in, w_out)


def _inproj_kernel(x_ref, mod_ref, nw_ref, w_ref, cr_ref, sr_ref, cd_ref, sa_ref, sb_ref,
                   qr_ref, kr_ref, vr_ref, rg_ref, qd_ref, kd_ref, vd_ref, gt_ref):
    x = x_ref[...]
    h = _rms(x, nw_ref[2:3, :]) * (1.0 + mod_ref[4:5, :]) + mod_ref[3:4, :]
    hb = h.astype(BF16)
    cr, sr = cr_ref[...], sr_ref[...]
    cd, sa, sb = cd_ref[...], sa_ref[...], sb_ref[...]

    rq = _mm(hb, w_ref[:, OFF_RQ:OFF_RK])
    rk = _mm(hb, w_ref[:, OFF_RK:OFF_RV])
    for hd in range(H_RET):
        sl = slice(hd * RET_DK, (hd + 1) * RET_DK)
        q = rq[:, sl]
        k = rk[:, sl]
        qr_ref[:, sl] = (q * cr + pltpu.roll(q, 64, 1) * sr).astype(BF16)
        kr_ref[:, sl] = ((k * cr + pltpu.roll(k, 64, 1) * sr) * (RET_DK ** -0.5)).astype(BF16)
    vr_ref[...] = _mm(hb, w_ref[:, OFF_RV:OFF_RG]).astype(BF16)
    rg_ref[...] = _mm(hb, w_ref[:, OFF_RG:OFF_DQ]).astype(BF16)

    dq = _mm(hb, w_ref[:, OFF_DQ:OFF_DK])
    dk = _mm(hb, w_ref[:, OFF_DK:OFF_DV])
    dv = _mm(hb, w_ref[:, OFF_DV:OFF_GT])
    for hd in range(H_DIFF):
        sl = slice(hd * DIFF_DV, (hd + 1) * DIFF_DV)
        q = dq[:, sl]
        k = dk[:, sl]
        qrot = q * cd + pltpu.roll(q, 16, 1) * sa + pltpu.roll(q, 112, 1) * sb
        krot = k * cd + pltpu.roll(k, 16, 1) * sa + pltpu.roll(k, 112, 1) * sb
        qd_ref[hd] = (qrot * (DIFF_DH ** -0.5 * LOG2E)).astype(BF16)
        kd_ref[hd] = krot.astype(BF16)
        vd_ref[hd] = dv[:, sl].astype(BF16)
    gt_ref[...] = _mm(hb, w_ref[:, OFF_GT:IN_WIDTH]).astype(BF16)


def _inproj(xa, modsel, nw, w_in, tables, *, ctx_tile):
    bsz, n, _ = xa.shape
    n_tiles = n // TM
    tok = lambda width: pl.BlockSpec((None, TM, width), lambda b, t: (b, t, 0))
    tab = pl.BlockSpec((TM, 128), lambda b, t: (t, 0))
    heads = pl.BlockSpec((None, H_DIFF, TM, DIFF_DV), lambda b, t: (b, 0, t, 0))
    sds = jax.ShapeDtypeStruct
    return pl.pallas_call(
        _inproj_kernel,
        grid=(bsz, n_tiles),
        in_specs=[
            tok(D_MODEL),
            pl.BlockSpec((None, None, N_MOD, D_MODEL), lambda b, t: (b, t // ctx_tile, 0, 0)),
            _resident((6, D_MODEL), lambda b, t: (0, 0)),
            _resident((D_MODEL, IN_WIDTH), lambda b, t: (0, 0)),
            tab, tab, tab, tab, tab,
        ],
        out_specs=[tok(512), tok(512), tok(1024), tok(1024), heads, heads, heads, tok(2048)],
        out_shape=[
            sds((bsz, n, 512), BF16), sds((bsz, n, 512), BF16),
            sds((bsz, n, 1024), BF16), sds((bsz, n, 1024), BF16),
            sds((bsz, H_DIFF, n, DIFF_DV), BF16), sds((bsz, H_DIFF, n, DIFF_DV), BF16),
            sds((bsz, H_DIFF, n, DIFF_DV), BF16), sds((bsz, n, 2048), BF16),
        ],
        compiler_params=_cparams(("arbitrary", "arbitrary")),
        name="inproj",
    )(xa, modsel, nw, w_in, *tables)


def _retention_kernel(dl_ref, qf_ref, kf_ref, vf_ref, qb_ref, kb_ref, vb_ref, yf_ref, yb_ref,
                      st_ref, dm_ref, zt_ref, xi_ref):
    c = RET_BLOCK
    dl = dl_ref[...]
    lg = jnp.minimum(dl, 0.0) - jnp.log(1.0 + jnp.exp(-jnp.abs(dl)))

    @pl.when(pl.program_id(1) == 0)
    def _init():
        st_ref[...] = jnp.zeros_like(st_ref)
        ri = lax.broadcasted_iota(jnp.int32, (c, c), 0).astype(F32)
        ci = lax.broadcasted_iota(jnp.int32, (c, c), 1).astype(F32)
        pos = lax.broadcasted_iota(jnp.int32, (c, RET_DK), 0).astype(F32)
        for d in range(2):
            diff = ri - ci if d == 0 else ci - ri
            for hd in range(H_RET):
                g = lg[d:d + 1, hd:hd + 1]
                dm_ref[d, hd] = jnp.where(diff >= 0.0, jnp.exp(jnp.maximum(diff, 0.0) * g), 0.0)
                if d == 0:
                    zt_ref[d, hd] = jnp.exp((c - 1.0 - pos) * g)
                    xi_ref[d, hd] = jnp.exp((pos + 1.0) * g)
                else:
                    zt_ref[d, hd] = jnp.exp(pos * g)
                    xi_ref[d, hd] = jnp.exp((c - pos) * g)

    for d, (q_ref, k_ref, v_ref, y_ref) in enumerate(
            ((qf_ref, kf_ref, vf_ref, yf_ref), (qb_ref, kb_ref, vb_ref, yb_ref))):
        for hd in range(H_RET):
            g_blk = jnp.exp(float(c) * lg[d:d + 1, hd:hd + 1])
            q = q_ref[:, hd * RET_DK:(hd + 1) * RET_DK]
            k = k_ref[:, hd * RET_DK:(hd + 1) * RET_DK]
            v = v_ref[:, hd * RET_DV:(hd + 1) * RET_DV]
            s = _mm_nt(q, k) * dm_ref[d, hd]
            inner = _mm(s.astype(BF16), v)
            state = st_ref[d, hd]
            qx = (q.astype(F32) * xi_ref[d, hd]).astype(BF16)
            cross = _mm(qx, state.astype(BF16))
            y_ref[:, hd * RET_DV:(hd + 1) * RET_DV] = (inner + cross).astype(BF16)
            kz = (k.astype(F32) * zt_ref[d, hd]).astype(BF16)
            st_ref[d, hd] = g_blk * state + _mm_tn(kz, v)


def _retention(decay_logit, q, k, v):
    bsz, n, _ = q.shape
    nb = n // RET_BLOCK
    last = nb - 1
    fwd = lambda b, i: (b, jnp.where(i == 0, last, i - 1), 0)
    bwd = lambda b, i: (b, jnp.where(i == 0, last, last - i), 0)
    blk = lambda width, im: pl.BlockSpec((None, RET_BLOCK, width), im)
    y = jax.ShapeDtypeStruct((bsz, n, H_RET * RET_DV), BF16)
    return pl.pallas_call(
        _retention_kernel,
        grid=(bsz, nb),
        in_specs=[
            pl.BlockSpec((2, H_RET), lambda b, i: (0, 0)),
            blk(512, fwd), blk(512, fwd), blk(1024, fwd),
            blk(512, bwd), blk(512, bwd), blk(1024, bwd),
        ],
        out_specs=[blk(1024, fwd), blk(1024, bwd)],
        out_shape=[y, y],
        scratch_shapes=[
            pltpu.VMEM((2, H_RET, RET_DK, RET_DV), F32),
            pltpu.VMEM((2, H_RET, RET_BLOCK, RET_BLOCK), F32),
            pltpu.VMEM((2, H_RET, RET_BLOCK, RET_DK), F32),
            pltpu.VMEM((2, H_RET, RET_BLOCK, RET_DK), F32),
        ],
        compiler_params=_cparams(("arbitrary", "arbitrary")),
        name="retention",
    )(decay_logit, q, k, v, q, k, v)


def _attn_kernel(lam_ref, sw_ref, q_ref, k_ref, v_ref, o_ref, qq_ref, m_ref, l_ref, acc_ref, s_ref, cm_ref,
                 *, tq, tk, n_chunks, lam_init):
    qt = q_ref[...].astype(F32).T
    row = lax.broadcasted_iota(jnp.int32, qt.shape, 0)
    qq_ref[:, :tq] = jnp.where(row < DIFF_DH, qt, 0.0).astype(BF16)
    qq_ref[:, tq:] = jnp.where(row >= DIFF_DH, qt, 0.0).astype(BF16)
    m_ref[...] = jnp.full_like(m_ref, -jnp.inf)
    l_ref[...] = jnp.zeros_like(l_ref)
    acc_ref[...] = jnp.zeros_like(acc_ref)

    def scores(ci, slot):
        start = pl.multiple_of(ci * tk, tk)
        s = _mm(k_ref[pl.ds(start, tk), :], qq_ref[...])
        s_ref[slot] = s
        cm_ref[slot] = jnp.max(s, axis=0, keepdims=True)

    def softmax_pv(ci, slot):
        start = pl.multiple_of(ci * tk, tk)
        vc = v_ref[pl.ds(start, tk), :]
        s = s_ref[slot]
        m_old = m_ref[...]
        m_new = jnp.maximum(m_old, cm_ref[slot])
        alpha = jnp.exp2(m_old - m_new)
        p = jnp.exp2(s - m_new)
        l_ref[...] = alpha * l_ref[...] + jnp.sum(p, axis=0, keepdims=True)
        acc_ref[...] = alpha * acc_ref[...] + _mm_tn(vc, p.astype(BF16))
        m_ref[...] = m_new

    scores(0, 0)
    n_loop = (n_chunks - 1) // ATT_UNROLL

    def body(j, carry):
        c0 = ATT_UNROLL * j
        for u in range(ATT_UNROLL):
            scores(c0 + u + 1, (u + 1) % 2)
            softmax_pv(c0 + u, u % 2)
        return carry

    if n_loop > 0:
        lax.fori_loop(0, n_loop, body, 0)
    for c in range(n_loop * ATT_UNROLL, n_chunks):
        if c + 1 < n_chunks:
            scores(c + 1, (c + 1) % 2)
        softmax_pv(c, c % 2)

    lv = lam_ref[...]
    lam = (jnp.exp(jnp.sum(lv[0:1, :] * lv[1:2, :], axis=1, keepdims=True))
           - jnp.exp(jnp.sum(lv[2:3, :] * lv[3:4, :], axis=1, keepdims=True)) + lam_init)
    acc = acc_ref[...]
    l = l_ref[...]
    o = acc[:, :tq] / l[:, :tq] - lam * (acc[:, tq:] / l[:, tq:])
    o = o * lax.rsqrt(jnp.mean(o * o, axis=0, keepdims=True) + EPS) * sw_ref[...] * (1.0 - lam_init)
    o_ref[...] = o.T.astype(BF16)


def _attention(lam_vec, subln_w, q, k, v, *, tq, tk, q_block0, n_q, kv_block0, n_kv, lam_init):
    bsz = q.shape[0]
    n_chunks = n_kv // tk
    kern = functools.partial(_attn_kernel, tq=tq, tk=tk, n_chunks=n_chunks, lam_init=lam_init)
    return pl.pallas_call(
        kern,
        grid=(bsz, H_DIFF, n_q),
        in_specs=[
            pl.BlockSpec((4, DIFF_DH), lambda b, h, i: (0, 0)),
            pl.BlockSpec((DIFF_DV, 1), lambda b, h, i: (0, 0)),
            pl.BlockSpec((None, None, tq, DIFF_DV), lambda b, h, i: (b, h, q_block0 + i, 0)),
            pl.BlockSpec((None, None, n_kv, DIFF_DV), lambda b, h, i: (b, h, kv_block0, 0)),
            pl.BlockSpec((None, None, n_kv, DIFF_DV), lambda b, h, i: (b, h, kv_block0, 0)),
        ],
        out_specs=pl.BlockSpec((None, tq, DIFF_DV), lambda b, h, i: (b, i, h)),
        out_shape=jax.ShapeDtypeStruct((bsz, n_q * tq, H_DIFF * DIFF_DV), BF16),
        scratch_shapes=[
            pltpu.VMEM((DIFF_DV, 2 * tq), BF16),
            pltpu.VMEM((1, 2 * tq), F32),
            pltpu.VMEM((1, 2 * tq), F32),
            pltpu.VMEM((DIFF_DV, 2 * tq), F32),
            pltpu.VMEM((2, tk, 2 * tq), F32),
            pltpu.VMEM((2, 1, 2 * tq), F32),
        ],
        compiler_params=_cparams(("arbitrary", "arbitrary", "arbitrary")),
        name="diffattn",
    )(lam_vec, subln_w.reshape(DIFF_DV, 1), q, k, v)


def _merge_kernel(x_ref, mod_ref, nw_ref, gn_ref, yf_ref, yb_ref, rg_ref, al_ref, ac_ref, gt_ref, w_ref, o_ref,
                  *, ctx_tile):
    x = x_ref[...]
    is_ctx = pl.program_id(1) == ctx_tile
    diff = jnp.where(is_ctx, ac_ref[...], al_ref[...]).astype(F32)
    y = yf_ref[...].astype(F32) + yb_ref[...].astype(F32)
    rg = rg_ref[...].astype(F32)
    parts = []
    for hd in range(H_RET):
        sl = slice(hd * RET_DV, (hd + 1) * RET_DV)
        yh = y[:, sl]
        mu = jnp.mean(yh, axis=-1, keepdims=True)
        yc = yh - mu
        var = jnp.mean(yc * yc, axis=-1, keepdims=True)
        parts.append(yc * lax.rsqrt(var + EPS) * gn_ref[:, sl])
    ret = jnp.concatenate(parts, axis=1) * (rg * jax.nn.sigmoid(rg))
    ga = jax.nn.sigmoid(gt_ref[:, :D_MODEL].astype(F32))
    gb = jax.nn.sigmoid(gt_ref[:, D_MODEL:].astype(F32))
    z = (ga * ret + gb * diff).astype(BF16)
    yo = _mm(z, w_ref[...])
    o_ref[...] = x + mod_ref[5:6, :] * _rms(yo, nw_ref[3:4, :])


def _merge(xa, modsel, nw, gn_w, yf, yb, rg, at_lat, at_ctx, gates, w_out, *, ctx_tile):
    bsz, n, _ = xa.shape
    n_tiles = n // TM
    tok = lambda width: pl.BlockSpec((None, TM, width), lambda b, t: (b, t, 0))
    kern = functools.partial(_merge_kernel, ctx_tile=ctx_tile)
    return pl.pallas_call(
        kern,
        grid=(bsz, n_tiles),
        in_specs=[
            tok(D_MODEL),
            pl.BlockSpec((None, None, N_MOD, D_MODEL), lambda b, t: (b, t // ctx_tile, 0, 0)),
            _resident((6, D_MODEL), lambda b, t: (0, 0)),
            _resident((1, D_MODEL), lambda b, t: (0, 0)),
            tok(1024), tok(1024), tok(1024),
            pl.BlockSpec((None, TM, D_MODEL), lambda b, t: (b, jnp.minimum(t, ctx_tile - 1), 0)),
            pl.BlockSpec((None, TM, D_MODEL), lambda b, t: (b, 0, 0)),
            tok(2048),
            _resident((D_MODEL, D_MODEL), lambda b, t: (0, 0)),
        ],
        out_specs=tok(D_MODEL),
        out_shape=jax.ShapeDtypeStruct((bsz, n, D_MODEL), F32),
        compiler_params=_cparams(("arbitrary", "arbitrary")),
        name="merge",
    )(xa, modsel, nw, gn_w, yf, yb, rg, at_lat, at_ctx, gates, w_out)


def _rope_tables(t, n):
    pos = jnp.arange(n, dtype=jnp.int32)
    is_lat = (pos < t)[:, None]
    lane = jnp.arange(128)
    inv_r = ROPE_BASE ** (-jnp.arange(64, dtype=F32) / 64)
    ang = pos.astype(F32)[:, None] * inv_r[None, :]
    cos, sin = jnp.cos(ang), jnp.sin(ang)
    cr = jnp.where(is_lat, jnp.concatenate([cos, cos], axis=1), 1.0)
    sr = jnp.where(is_lat, jnp.concatenate([-sin, sin], axis=1), 0.0)
    inv_d = ROPE_BASE ** (-jnp.arange(16, dtype=F32) / 16)
    row = (pos // GRID_W).astype(F32)[:, None] * inv_d[None, :]
    col = (pos % GRID_W).astype(F32)[:, None] * inv_d[None, :]
    cos64 = jnp.concatenate([jnp.cos(row)] * 2 + [jnp.cos(col)] * 2, axis=1)
    sin64 = jnp.concatenate([jnp.sin(row)] * 2 + [jnp.sin(col)] * 2, axis=1)
    cos128 = jnp.concatenate([cos64, cos64], axis=1)
    sin128 = jnp.concatenate([sin64, sin64], axis=1)
    upper = ((lane % 32) >= 16)[None, :]
    cd = jnp.where(is_lat, cos128, 1.0)
    sa = jnp.where(is_lat & upper, sin128, 0.0)
    sb = jnp.where(is_lat & ~upper, -sin128, 0.0)
    return cr, sr, cd, sa, sb


def kernel(x, c, ctx, c_ctx, w_ada, b_ada, norm_w, ffn_w_in, ffn_w_out, w_in, w_out,
           ret_decay_logit, ret_gn_w, diff_lambda, diff_subln_w):
    bsz, t, d = x.shape
    n_ctx = ctx.shape[1]
    depth = w_ada.shape[0]
    assert d == D_MODEL and n_ctx == TM and t % ATT_TQ == 0 and bsz + 1 <= 8
    n = t + n_ctx
    assert n % ATT_TK == 0
    n_tiles = n // TM
    ctx_tile = n_tiles - 1

    xa = jnp.concatenate([x, ctx], axis=1)
    cs = jnp.zeros((8, d), F32).at[:bsz].set(c).at[bsz].set(c_ctx)
    mod = _adaln(cs, w_ada, b_ada).reshape(depth, 8, N_MOD, d)
    modsel = jnp.stack([mod[:, :bsz], jnp.broadcast_to(mod[:, bsz:bsz + 1], (depth, bsz, N_MOD, d))], axis=2)
    tables = _rope_tables(t, n)

    ffn_w_in_b = ffn_w_in.astype(BF16)
    ffn_w_out_b = ffn_w_out.astype(BF16)
    w_in_b = w_in.astype(BF16)
    w_out_b = w_out.astype(BF16)

    out = None
    for l in range(depth):
        last = l == depth - 1
        lam_init = 0.8 - 0.6 * math.exp(-0.3 * l)
        nw = norm_w[l]
        xa = _ffn(xa, modsel[l], nw, ffn_w_in_b[l, 0], ffn_w_out_b[l, 0],
                  mod_off=0, nw_off=0, n_tiles=n_tiles, ctx_tile=ctx_tile)
        qr, kr, vr, rg, qd, kd, vd, gates = _inproj(xa, modsel[l], nw, w_in_b[l], tables, ctx_tile=ctx_tile)
        yf, yb = _retention(ret_decay_logit[l], qr, kr, vr)
        at_lat = _attention(diff_lambda[l], diff_subln_w[l], qd, kd, vd, tq=ATT_TQ, tk=ATT_TK,
                            q_block0=0, n_q=t // ATT_TQ, kv_block0=0, n_kv=n, lam_init=lam_init)
        at_ctx = _attention(diff_lambda[l], diff_subln_w[l], qd, kd, vd, tq=TM, tk=TM,
                            q_block0=ctx_tile, n_q=1, kv_block0=ctx_tile, n_kv=TM, lam_init=lam_init)
        xa = _merge(xa, modsel[l], nw, ret_gn_w[l].reshape(1, d), yf, yb, rg, at_lat, at_ctx, gates,
                    w_out_b[l], ctx_tile=ctx_tile)
        if last:
            out = _ffn(xa, modsel[l], nw, ffn_w_in_b[l, 1], ffn_w_out_b[l, 1],
                       mod_off=6, nw_off=4, n_tiles=t // TM, ctx_tile=ctx_tile)
        else:
            xa = _ffn(xa, modsel[l], nw, ffn_w_in_b[l, 1], ffn_w_out_b[l, 1],
                      mod_off=6, nw_off=4, n_tiles=n_tiles, ctx_tile=ctx_tile)
    return out
```

```python
import functools
import math

import jax
import jax.numpy as jnp
from jax import lax
from jax.experimental import pallas as pl
from jax.experimental.pallas import tpu as pltpu

D_MODEL = 1024
D_FF = 2816
H_RET = 4
RET_DK = 128
RET_DV = 256
H_DIFF = 8
DIFF_DV = 128
DIFF_DH = 64
GRID_W = 64
ROPE_BASE = 10000.0
EPS = 1e-6
LOG2E = 1.4426950408889634
N_MOD = 9
IN_WIDTH = 8192
OFF_RQ, OFF_RK, OFF_RV, OFF_RG, OFF_DQ, OFF_DK, OFF_DV, OFF_GT = 0, 512, 1024, 2048, 3072, 4096, 5120, 6144

TM = 256
RET_BLOCK = 256
ATT_TQ = 512
ATT_TK = 1280
ATT_UNROLL = 4
ATT_RB = 256
VMEM_LIMIT = 56 * 1024 * 1024

BF16 = jnp.bfloat16
F32 = jnp.float32


def _cparams(sem):
    return pltpu.CompilerParams(dimension_semantics=sem, vmem_limit_bytes=VMEM_LIMIT)


def _resident(shape, index_map):
    return pl.BlockSpec(shape, index_map, pipeline_mode=pl.Buffered(1))


def _rms(x, g):
    return x * lax.rsqrt(jnp.mean(x * x, axis=-1, keepdims=True) + EPS) * g


def _mm(a, b):
    return jnp.dot(a, b, preferred_element_type=F32)


def _mm_nt(a, b):
    return lax.dot_general(a, b, (((1,), (1,)), ((), ())), preferred_element_type=F32)


def _mm_tn(a, b):
    return lax.dot_general(a, b, (((0,), (0,)), ((), ())), preferred_element_type=F32)


def _adaln_kernel(c_ref, w_ref, b_ref, o_ref):
    c = c_ref[...]
    s = c * jax.nn.sigmoid(c)
    o_ref[...] = _mm(s, w_ref[...]) + b_ref[...]


def _adaln(cs, w_ada, b_ada):
    depth = w_ada.shape[0]
    width = w_ada.shape[2]
    tn = 1024
    return pl.pallas_call(
        _adaln_kernel,
        grid=(depth, width // tn),
        in_specs=[
            pl.BlockSpec((8, D_MODEL), lambda l, j: (0, 0)),
            pl.BlockSpec((None, D_MODEL, tn), lambda l, j: (l, 0, j)),
            pl.BlockSpec((None, 1, tn), lambda l, j: (l, 0, j)),
        ],
        out_specs=pl.BlockSpec((None, 8, tn), lambda l, j: (l, 0, j)),
        out_shape=jax.ShapeDtypeStruct((depth, 8, width), F32),
        compiler_params=_cparams(("arbitrary", "arbitrary")),
        name="adaln",
    )(cs, w_ada, b_ada.reshape(depth, 1, width))


def _ffn_kernel(x_ref, mod_ref, nw_ref, win_ref, wout_ref, o_ref, *, mod_off, nw_off):
    x = x_ref[...]
    shift = mod_ref[mod_off:mod_off + 1, :]
    scale = mod_ref[mod_off + 1:mod_off + 2, :]
    gate = mod_ref[mod_off + 2:mod_off + 3, :]
    h = _rms(x, nw_ref[nw_off:nw_off + 1, :]) * (1.0 + scale) + shift
    hb = h.astype(BF16)
    a = _mm(hb, win_ref[:, :D_FF])
    b = _mm(hb, win_ref[:, D_FF:])
    u = (a * jax.nn.sigmoid(a) * b).astype(BF16)
    y = _mm(u, wout_ref[...])
    o_ref[...] = x + 0.5 * gate * _rms(y, nw_ref[nw_off + 1:nw_off + 2, :])


def _ffn(xa, modsel, nw, w_in, w_out, *, mod_off, nw_off, n_tiles, ctx_tile):
    bsz = xa.shape[0]
    kern = functools.partial(_ffn_kernel, mod_off=mod_off, nw_off=nw_off)
    return pl.pallas_call(
        kern,
        grid=(bsz, n_tiles),
        in_specs=[
            pl.BlockSpec((None, TM, D_MODEL), lambda b, t: (b, t, 0)),
            pl.BlockSpec((None, None, N_MOD, D_MODEL), lambda b, t: (b, t // ctx_tile, 0, 0)),
            _resident((6, D_MODEL), lambda b, t: (0, 0)),
            _resident((D_MODEL, 2 * D_FF), lambda b, t: (0, 0)),
            _resident((D_FF, D_MODEL), lambda b, t: (0, 0)),
        ],
        out_specs=pl.BlockSpec((None, TM, D_MODEL), lambda b, t: (b, t, 0)),
        out_shape=jax.ShapeDtypeStruct((bsz, n_tiles * TM, D_MODEL), F32),
        compiler_params=_cparams(("arbitrary", "arbitrary")),
        name="ffn",
    )(xa, modsel, nw, w_in, w_out)


def _inproj_kernel(x_ref, mod_ref, nw_ref, w_ref, cr_ref, sr_ref, cd_ref, sa_ref, sb_ref,
                   qr_ref, kr_ref, vr_ref, rg_ref, qd_ref, kd_ref, vd_ref, gt_ref):
    x = x_ref[...]
    h = _rms(x, nw_ref[2:3, :]) * (1.0 + mod_ref[4:5, :]) + mod_ref[3:4, :]
    hb = h.astype(BF16)
    cr, sr = cr_ref[...], sr_ref[...]
    cd, sa, sb = cd_ref[...], sa_ref[...], sb_ref[...]

    rq = _mm(hb, w_ref[:, OFF_RQ:OFF_RK])
    rk = _mm(hb, w_ref[:, OFF_RK:OFF_RV])
    for hd in range(H_RET):
        sl = slice(hd * RET_DK, (hd + 1) * RET_DK)
        q = rq[:, sl]
        k = rk[:, sl]
        qr_ref[:, sl] = (q * cr + pltpu.roll(q, 64, 1) * sr).astype(BF16)
        kr_ref[:, sl] = ((k * cr + pltpu.roll(k, 64, 1) * sr) * (RET_DK ** -0.5)).astype(BF16)
    vr_ref[...] = _mm(hb, w_ref[:, OFF_RV:OFF_RG]).astype(BF16)
    rg_ref[...] = _mm(hb, w_ref[:, OFF_RG:OFF_DQ]).astype(BF16)

    dq = _mm(hb, w_ref[:, OFF_DQ:OFF_DK])
    dk = _mm(hb, w_ref[:, OFF_DK:OFF_DV])
    dv = _mm(hb, w_ref[:, OFF_DV:OFF_GT])
    for hd in range(H_DIFF):
        sl = slice(hd * DIFF_DV, (hd + 1) * DIFF_DV)
        q = dq[:, sl]
        k = dk[:, sl]
        qrot = q * cd + pltpu.roll(q, 16, 1) * sa + pltpu.roll(q, 112, 1) * sb
        krot = k * cd + pltpu.roll(k, 16, 1) * sa + pltpu.roll(k, 112, 1) * sb
        qd_ref[hd] = (qrot * (DIFF_DH ** -0.5 * LOG2E)).astype(BF16)
        kd_ref[hd] = krot.astype(BF16)
        vd_ref[hd] = dv[:, sl].astype(BF16)
    gt_ref[...] = _mm(hb, w_ref[:, OFF_GT:IN_WIDTH]).astype(BF16)


def _inproj(xa, modsel, nw, w_in, tables, *, ctx_tile):
    bsz, n, _ = xa.shape
    n_tiles = n // TM
    tok = lambda width: pl.BlockSpec((None, TM, width), lambda b, t: (b, t, 0))
    tab = pl.BlockSpec((TM, 128), lambda b, t: (t, 0))
    heads = pl.BlockSpec((None, H_DIFF, TM, DIFF_DV), lambda b, t: (b, 0, t, 0))
    sds = jax.ShapeDtypeStruct
    return pl.pallas_call(
        _inproj_kernel,
        grid=(bsz, n_tiles),
        in_specs=[
            tok(D_MODEL),
            pl.BlockSpec((None, None, N_MOD, D_MODEL), lambda b, t: (b, t // ctx_tile, 0, 0)),
            _resident((6, D_MODEL), lambda b, t: (0, 0)),
            _resident((D_MODEL, IN_WIDTH), lambda b, t: (0, 0)),
            tab, tab, tab, tab, tab,
        ],
        out_specs=[tok(512), tok(512), tok(1024), tok(1024), heads, heads, heads, tok(2048)],
        out_shape=[
            sds((bsz, n, 512), BF16), sds((bsz, n, 512), BF16),
            sds((bsz, n, 1024), BF16), sds((bsz, n, 1024), BF16),
            sds((bsz, H_DIFF, n, DIFF_DV), BF16), sds((bsz, H_DIFF, n, DIFF_DV), BF16),
            sds((bsz, H_DIFF, n, DIFF_DV), BF16), sds((bsz, n, 2048), BF16),
        ],
        compiler_params=_cparams(("arbitrary", "arbitrary")),
        name="inproj",
    )(xa, modsel, nw, w_in, *tables)


def _retention_kernel(dl_ref, qf_ref, kf_ref, vf_ref, qb_ref, kb_ref, vb_ref, yf_ref, yb_ref,
                      st_ref, dm_ref, zt_ref, xi_ref):
    c = RET_BLOCK
    dl = dl_ref[...]
    lg = jnp.minimum(dl, 0.0) - jnp.log(1.0 + jnp.exp(-jnp.abs(dl)))

    @pl.when(pl.program_id(0) == 0)
    def _init():
        st_ref[...] = jnp.zeros_like(st_ref)
        ri = lax.broadcasted_iota(jnp.int32, (c, c), 0).astype(F32)
        ci = lax.broadcasted_iota(jnp.int32, (c, c), 1).astype(F32)
        pos = lax.broadcasted_iota(jnp.int32, (c, RET_DK), 0).astype(F32)
        for d in range(2):
            diff = ri - ci if d == 0 else ci - ri
            for hd in range(H_RET):
                g = lg[d:d + 1, hd:hd + 1]
                dm_ref[d, hd] = jnp.where(diff >= 0.0, jnp.exp(jnp.maximum(diff, 0.0) * g), 0.0)
                if d == 0:
                    zt_ref[d, hd] = jnp.exp((c - 1.0 - pos) * g)
                    xi_ref[d, hd] = jnp.exp((pos + 1.0) * g)
                else:
                    zt_ref[d, hd] = jnp.exp(pos * g)
                    xi_ref[d, hd] = jnp.exp((c - pos) * g)

    for d, (q_ref, k_ref, v_ref, y_ref) in enumerate(
            ((qf_ref, kf_ref, vf_ref, yf_ref), (qb_ref, kb_ref, vb_ref, yb_ref))):
        for hd in range(H_RET):
            g_blk = jnp.exp(float(c) * lg[d:d + 1, hd:hd + 1])
            for b in range(q_ref.shape[0]):
                q = q_ref[b, :, hd * RET_DK:(hd + 1) * RET_DK]
                k = k_ref[b, :, hd * RET_DK:(hd + 1) * RET_DK]
                v = v_ref[b, :, hd * RET_DV:(hd + 1) * RET_DV]
                s = _mm_nt(q, k) * dm_ref[d, hd]
                inner = _mm(s.astype(BF16), v)
                state = st_ref[b, d, hd]
                qx = (q.astype(F32) * xi_ref[d, hd]).astype(BF16)
                cross = _mm(qx, state.astype(BF16))
                y_ref[b, :, hd * RET_DV:(hd + 1) * RET_DV] = (inner + cross).astype(BF16)
                kz = (k.astype(F32) * zt_ref[d, hd]).astype(BF16)
                st_ref[b, d, hd] = g_blk * state + _mm_tn(kz, v)


def _retention(decay_logit, q, k, v):
    bsz, n, _ = q.shape
    nb = n // RET_BLOCK
    last = nb - 1
    fwd = lambda i: (0, jnp.where(i == 0, last, i - 1), 0)
    bwd = lambda i: (0, jnp.where(i == 0, last, last - i), 0)
    blk = lambda width, im: pl.BlockSpec((bsz, RET_BLOCK, width), im)
    y = jax.ShapeDtypeStruct((bsz, n, H_RET * RET_DV), BF16)
    return pl.pallas_call(
        _retention_kernel,
        grid=(nb,),
        in_specs=[
            pl.BlockSpec((2, H_RET), lambda i: (0, 0)),
            blk(512, fwd), blk(512, fwd), blk(1024, fwd),
            blk(512, bwd), blk(512, bwd), blk(1024, bwd),
        ],
        out_specs=[blk(1024, fwd), blk(1024, bwd)],
        out_shape=[y, y],
        scratch_shapes=[
            pltpu.VMEM((bsz, 2, H_RET, RET_DK, RET_DV), F32),
            pltpu.VMEM((2, H_RET, RET_BLOCK, RET_BLOCK), F32),
            pltpu.VMEM((2, H_RET, RET_BLOCK, RET_DK), F32),
            pltpu.VMEM((2, H_RET, RET_BLOCK, RET_DK), F32),
        ],
        compiler_params=_cparams(("arbitrary",)),
        name="retention",
    )(decay_logit, q, k, v, q, k, v)


def _attn_kernel(lam_ref, sw_ref, q_ref, k_ref, v_ref, o_ref, qq_ref, m_ref, l_ref, acc_ref, s_ref, cm_ref,
                 *, tq, tk, n_chunks, lam_init):
    qt = q_ref[...].astype(F32).T
    row = lax.broadcasted_iota(jnp.int32, qt.shape, 0)
    qq_ref[:, :tq] = jnp.where(row < DIFF_DH, qt, 0.0).astype(BF16)
    qq_ref[:, tq:] = jnp.where(row >= DIFF_DH, qt, 0.0).astype(BF16)
    m_ref[...] = jnp.full_like(m_ref, -jnp.inf)
    l_ref[...] = jnp.zeros_like(l_ref)
    acc_ref[...] = jnp.zeros_like(acc_ref)

    def scores(ci, slot):
        start = pl.multiple_of(ci * tk, tk)
        s = _mm(k_ref[pl.ds(start, tk), :], qq_ref[...])
        s_ref[slot] = s
        cm_ref[slot] = jnp.max(s, axis=0, keepdims=True)

    def softmax_pv(ci, slot):
        start = pl.multiple_of(ci * tk, tk)
        vc = v_ref[pl.ds(start, tk), :]
        s = s_ref[slot]
        m_old = m_ref[...]
        m_new = jnp.maximum(m_old, cm_ref[slot])
        alpha = jnp.exp2(m_old - m_new)
        p = jnp.exp2(s - m_new)
        l_ref[...] = alpha * l_ref[...] + jnp.sum(p, axis=0, keepdims=True)
        acc_ref[...] = alpha * acc_ref[...] + _mm_tn(vc, p.astype(BF16))
        m_ref[...] = m_new

    scores(0, 0)
    n_loop = (n_chunks - 1) // ATT_UNROLL

    n_rb = tk // ATT_RB

    def fused(c_next, c_cur, slot):
        nstart = pl.multiple_of(c_next * tk, tk)
        cstart = pl.multiple_of(c_cur * tk, tk)
        m_old = m_ref[...]
        m_new = jnp.maximum(m_old, cm_ref[slot])
        alpha = jnp.exp2(m_old - m_new)
        lsum = alpha * l_ref[...]
        pv = None
        cmax = None
        for r in range(n_rb):
            rows = slice(r * ATT_RB, (r + 1) * ATT_RB)
            sn = _mm(k_ref[pl.ds(nstart + r * ATT_RB, ATT_RB), :], qq_ref[...])
            s_ref[1 - slot, rows, :] = sn
            cm_r = jnp.max(sn, axis=0, keepdims=True)
            cmax = cm_r if cmax is None else jnp.maximum(cmax, cm_r)
            p = jnp.exp2(s_ref[slot, rows, :] - m_new)
            lsum = lsum + jnp.sum(p, axis=0, keepdims=True)
            pv_r = _mm_tn(v_ref[pl.ds(cstart + r * ATT_RB, ATT_RB), :], p.astype(BF16))
            pv = pv_r if pv is None else pv + pv_r
        cm_ref[1 - slot] = cmax
        l_ref[...] = lsum
        acc_ref[...] = alpha * acc_ref[...] + pv
        m_ref[...] = m_new

    def body(j, carry):
        c0 = ATT_UNROLL * j
        for u in range(ATT_UNROLL):
            fused(c0 + u + 1, c0 + u, u % 2)
        return carry

    if n_loop > 0:
        lax.fori_loop(0, n_loop, body, 0)
    for c in range(n_loop * ATT_UNROLL, n_chunks):
        if c + 1 < n_chunks:
            scores(c + 1, (c + 1) % 2)
        softmax_pv(c, c % 2)

    lv = lam_ref[...]
    lam = (jnp.exp(jnp.sum(lv[0:1, :] * lv[1:2, :], axis=1, keepdims=True))
           - jnp.exp(jnp.sum(lv[2:3, :] * lv[3:4, :], axis=1, keepdims=True)) + lam_init)
    acc = acc_ref[...]
    l = l_ref[...]
    o = acc[:, :tq] / l[:, :tq] - lam * (acc[:, tq:] / l[:, tq:])
    o = o * lax.rsqrt(jnp.mean(o * o, axis=0, keepdims=True) + EPS) * sw_ref[...] * (1.0 - lam_init)
    o_ref[...] = o.T.astype(BF16)


def _attention(lam_vec, subln_w, q, k, v, *, tq, tk, q_block0, n_q, kv_block0, n_kv, lam_init):
    bsz = q.shape[0]
    n_chunks = n_kv // tk
    kern = functools.partial(_attn_kernel, tq=tq, tk=tk, n_chunks=n_chunks, lam_init=lam_init)
    return pl.pallas_call(
        kern,
        grid=(bsz, H_DIFF, n_q),
        in_specs=[
            pl.BlockSpec((4, DIFF_DH), lambda b, h, i: (0, 0)),
            pl.BlockSpec((DIFF_DV, 1), lambda b, h, i: (0, 0)),
            pl.BlockSpec((None, None, tq, DIFF_DV), lambda b, h, i: (b, h, q_block0 + i, 0)),
            pl.BlockSpec((None, None, n_kv, DIFF_DV), lambda b, h, i: (b, h, kv_block0, 0)),
            pl.BlockSpec((None, None, n_kv, DIFF_DV), lambda b, h, i: (b, h, kv_block0, 0)),
        ],
        out_specs=pl.BlockSpec((None, tq, DIFF_DV), lambda b, h, i: (b, i, h)),
        out_shape=jax.ShapeDtypeStruct((bsz, n_q * tq, H_DIFF * DIFF_DV), BF16),
        scratch_shapes=[
            pltpu.VMEM((DIFF_DV, 2 * tq), BF16),
            pltpu.VMEM((1, 2 * tq), F32),
            pltpu.VMEM((1, 2 * tq), F32),
            pltpu.VMEM((DIFF_DV, 2 * tq), F32),
            pltpu.VMEM((2, tk, 2 * tq), F32),
            pltpu.VMEM((2, 1, 2 * tq), F32),
        ],
        compiler_params=_cparams(("arbitrary", "arbitrary", "arbitrary")),
        name="diffattn",
    )(lam_vec, subln_w.reshape(DIFF_DV, 1), q, k, v)


def _merge_kernel(x_ref, mod_ref, nw_ref, gn_ref, yf_ref, yb_ref, rg_ref, al_ref, ac_ref, gt_ref, w_ref, o_ref,
                  *, ctx_tile):
    x = x_ref[...]
    is_ctx = pl.program_id(1) == ctx_tile
    diff = jnp.where(is_ctx, ac_ref[...], al_ref[...]).astype(F32)
    y = yf_ref[...].astype(F32) + yb_ref[...].astype(F32)
    rg = rg_ref[...].astype(F32)
    parts = []
    for hd in range(H_RET):
        sl = slice(hd * RET_DV, (hd + 1) * RET_DV)
        yh = y[:, sl]
        mu = jnp.mean(yh, axis=-1, keepdims=True)
        yc = yh - mu
        var = jnp.mean(yc * yc, axis=-1, keepdims=True)
        parts.append(yc * lax.rsqrt(var + EPS) * gn_ref[:, sl])
    ret = jnp.concatenate(parts, axis=1) * (rg * jax.nn.sigmoid(rg))
    ga = jax.nn.sigmoid(gt_ref[:, :D_MODEL].astype(F32))
    gb = jax.nn.sigmoid(gt_ref[:, D_MODEL:].astype(F32))
    z = (ga * ret + gb * diff).astype(BF16)
    yo = _mm(z, w_ref[...])
    o_ref[...] = x + mod_ref[5:6, :] * _rms(yo, nw_ref[3:4, :])


def _merge(xa, modsel, nw, gn_w, yf, yb, rg, at_lat, at_ctx, gates, w_out, *, ctx_tile):
    bsz, n, _ = xa.shape
    n_tiles = n // TM
    tok = lambda width: pl.BlockSpec((None, TM, width), lambda b, t: (b, t, 0))
    kern = functools.partial(_merge_kernel, ctx_tile=ctx_tile)
    return pl.pallas_call(
        kern,
        grid=(bsz, n_tiles),
        in_specs=[
            tok(D_MODEL),
            pl.BlockSpec((None, None, N_MOD, D_MODEL), lambda b, t: (b, t // ctx_tile, 0, 0)),
            _resident((6, D_MODEL), lambda b, t: (0, 0)),
            _resident((1, D_MODEL), lambda b, t: (0, 0)),
            tok(1024), tok(1024), tok(1024),
            pl.BlockSpec((None, TM, D_MODEL), lambda b, t: (b, jnp.minimum(t, ctx_tile - 1), 0)),
            pl.BlockSpec((None, TM, D_MODEL), lambda b, t: (b, 0, 0)),
            tok(2048),
            _resident((D_MODEL, D_MODEL), lambda b, t: (0, 0)),
        ],
        out_specs=tok(D_MODEL),
        out_shape=jax.ShapeDtypeStruct((bsz, n, D_MODEL), F32),
        compiler_params=_cparams(("arbitrary", "arbitrary")),
        name="merge",
    )(xa, modsel, nw, gn_w, yf, yb, rg, at_lat, at_ctx, gates, w_out)


def _rope_tables(t, n):
    pos = jnp.arange(n, dtype=jnp.int32)
    is_lat = (pos < t)[:, None]
    lane = jnp.arange(128)
    inv_r = ROPE_BASE ** (-jnp.arange(64, dtype=F32) / 64)
    ang = pos.astype(F32)[:, None] * inv_r[None, :]
    cos, sin = jnp.cos(ang), jnp.sin(ang)
    cr = jnp.where(is_lat, jnp.concatenate([cos, cos], axis=1), 1.0)
    sr = jnp.where(is_lat, jnp.concatenate([-sin, sin], axis=1), 0.0)
    inv_d = ROPE_BASE ** (-jnp.arange(16, dtype=F32) / 16)
    row = (pos // GRID_W).astype(F32)[:, None] * inv_d[None, :]
    col = (pos % GRID_W).astype(F32)[:, None] * inv_d[None, :]
    cos64 = jnp.concatenate([jnp.cos(row)] * 2 + [jnp.cos(col)] * 2, axis=1)
    sin64 = jnp.concatenate([jnp.sin(row)] * 2 + [jnp.sin(col)] * 2, axis=1)
    cos128 = jnp.concatenate([cos64, cos64], axis=1)
    sin128 = jnp.concatenate([sin64, sin64], axis=1)
    upper = ((lane % 32) >= 16)[None, :]
    cd = jnp.where(is_lat, cos128, 1.0)
    sa = jnp.where(is_lat & upper, sin128, 0.0)
    sb = jnp.where(is_lat & ~upper, -sin128, 0.0)
    return cr, sr, cd, sa, sb


def kernel(x, c, ctx, c_ctx, w_ada, b_ada, norm_w, ffn_w_in, ffn_w_out, w_in, w_out,
           ret_decay_logit, ret_gn_w, diff_lambda, diff_subln_w):
    bsz, t, d = x.shape
    n_ctx = ctx.shape[1]
    depth = w_ada.shape[0]
    assert d == D_MODEL and n_ctx == TM and t % ATT_TQ == 0 and bsz + 1 <= 8
    n = t + n_ctx
    assert n % ATT_TK == 0
    n_tiles = n // TM
    ctx_tile = n_tiles - 1

    xa = jnp.concatenate([x, ctx], axis=1)
    cs = jnp.zeros((8, d), F32).at[:bsz].set(c).at[bsz].set(c_ctx)
    mod = _adaln(cs, w_ada, b_ada).reshape(depth, 8, N_MOD, d)
    modsel = jnp.stack([mod[:, :bsz], jnp.broadcast_to(mod[:, bsz:bsz + 1], (depth, bsz, N_MOD, d))], axis=2)
    tables = _rope_tables(t, n)

    ffn_w_in_b = ffn_w_in.astype(BF16)
    ffn_w_out_b = ffn_w_out.astype(BF16)
    w_in_b = w_in.astype(BF16)
    w_out_b = w_out.astype(BF16)

    out = None
    for l in range(depth):
        last = l == depth - 1
        lam_init = 0.8 - 0.6 * math.exp(-0.3 * l)
        nw = norm_w[l]
        xa = _ffn(xa, modsel[l], nw, ffn_w_in_b[l, 0], ffn_w_out_b[l, 0],
                  mod_off=0, nw_off=0, n_tiles=n_tiles, ctx_tile=ctx_tile)
        qr, kr, vr, rg, qd, kd, vd, gates = _inproj(xa, modsel[l], nw, w_in_b[l], tables, ctx_tile=ctx_tile)
        yf, yb = _retention(ret_decay_logit[l], qr, kr, vr)
        at_lat = _attention(diff_lambda[l], diff_subln_w[l], qd, kd, vd, tq=ATT_TQ, tk=ATT_TK,
                            q_block0=0, n_q=t // ATT_TQ, kv_block0=0, n_kv=n, lam_init=lam_init)
        at_ctx = _attention(diff_lambda[l], diff_subln_w[l], qd, kd, vd, tq=TM, tk=TM,
                            q_block0=ctx_tile, n_q=1, kv_block0=ctx_tile, n_kv=TM, lam_init=lam_init)
        xa = _merge(xa, modsel[l], nw, ret_gn_w[l].reshape(1, d), yf, yb, rg, at_lat, at_ctx, gates,
                    w_out_b[l], ctx_tile=ctx_tile)
        if last:
            out = _ffn(xa, modsel[l], nw, ffn_w_in_b[l, 1], ffn_w_out_b[l, 1],
                       mod_off=6, nw_off=4, n_tiles=t // TM, ctx_tile=ctx_tile)
        else:
            xa = _ffn(xa, modsel[l], nw, ffn_w_in_b[l, 1], ffn_w_out_b[l, 1],
                      mod_off=6, nw_off=4, n_tiles=n_tiles, ctx_tile=ctx_tile)
    return out
```

```python
import functools
import math

import jax
import jax.numpy as jnp
from jax import lax
from jax.experimental import pallas as pl
from jax.experimental.pallas import tpu as pltpu

D_MODEL = 1024
D_FF = 2816
H_RET = 4
RET_DK = 128
RET_DV = 256
H_DIFF = 8
DIFF_DV = 128
DIFF_DH = 64
GRID_W = 64
ROPE_BASE = 10000.0
EPS = 1e-6
LOG2E = 1.4426950408889634
N_MOD = 9
IN_WIDTH = 8192
OFF_RQ, OFF_RK, OFF_RV, OFF_RG, OFF_DQ, OFF_DK, OFF_DV, OFF_GT = 0, 512, 1024, 2048, 3072, 4096, 5120, 6144

TM = 256
RET_BLOCK = 256
ATT_TQ = 512
ATT_TK = 1280
ATT_UNROLL = 4
ATT_RB = 256
ATT_VROWS = 144
VMEM_LIMIT = 56 * 1024 * 1024

BF16 = jnp.bfloat16
F32 = jnp.float32


def _cparams(sem):
    return pltpu.CompilerParams(dimension_semantics=sem, vmem_limit_bytes=VMEM_LIMIT)


def _resident(shape, index_map):
    return pl.BlockSpec(shape, index_map, pipeline_mode=pl.Buffered(1))


def _rms(x, g):
    return x * lax.rsqrt(jnp.mean(x * x, axis=-1, keepdims=True) + EPS) * g


def _mm(a, b):
    return jnp.dot(a, b, preferred_element_type=F32)


def _mm_nt(a, b):
    return lax.dot_general(a, b, (((1,), (1,)), ((), ())), preferred_element_type=F32)


def _mm_tn(a, b):
    return lax.dot_general(a, b, (((0,), (0,)), ((), ())), preferred_element_type=F32)


def _adaln_kernel(c_ref, w_ref, b_ref, o_ref):
    c = c_ref[...]
    s = c * jax.nn.sigmoid(c)
    o_ref[...] = _mm(s, w_ref[...]) + b_ref[...]


def _adaln(cs, w_ada, b_ada):
    depth = w_ada.shape[0]
    width = w_ada.shape[2]
    tn = 1024
    return pl.pallas_call(
        _adaln_kernel,
        grid=(depth, width // tn),
        in_specs=[
            pl.BlockSpec((8, D_MODEL), lambda l, j: (0, 0)),
            pl.BlockSpec((None, D_MODEL, tn), lambda l, j: (l, 0, j)),
            pl.BlockSpec((None, 1, tn), lambda l, j: (l, 0, j)),
        ],
        out_specs=pl.BlockSpec((None, 8, tn), lambda l, j: (l, 0, j)),
        out_shape=jax.ShapeDtypeStruct((depth, 8, width), F32),
        compiler_params=_cparams(("arbitrary", "arbitrary")),
        name="adaln",
    )(cs, w_ada, b_ada.reshape(depth, 1, width))


def _ffn_kernel(x_ref, mod_ref, nw_ref, win_ref, wout_ref, o_ref, *, mod_off, nw_off):
    x = x_ref[...]
    shift = mod_ref[mod_off:mod_off + 1, :]
    scale = mod_ref[mod_off + 1:mod_off + 2, :]
    gate = mod_ref[mod_off + 2:mod_off + 3, :]
    h = _rms(x, nw_ref[nw_off:nw_off + 1, :]) * (1.0 + scale) + shift
    hb = h.astype(BF16)
    a = _mm(hb, win_ref[:, :D_FF])
    b = _mm(hb, win_ref[:, D_FF:])
    u = (a * jax.nn.sigmoid(a) * b).astype(BF16)
    y = _mm(u, wout_ref[...])
    o_ref[...] = x + 0.5 * gate * _rms(y, nw_ref[nw_off + 1:nw_off + 2, :])


def _ffn(xa, modsel, nw, w_in, w_out, *, mod_off, nw_off, n_tiles, ctx_tile):
    bsz = xa.shape[0]
    kern = functools.partial(_ffn_kernel, mod_off=mod_off, nw_off=nw_off)
    return pl.pallas_call(
        kern,
        grid=(bsz, n_tiles),
        in_specs=[
            pl.BlockSpec((None, TM, D_MODEL), lambda b, t: (b, t, 0)),
            pl.BlockSpec((None, None, N_MOD, D_MODEL), lambda b, t: (b, t // ctx_tile, 0, 0)),
            _resident((6, D_MODEL), lambda b, t: (0, 0)),
            _resident((D_MODEL, 2 * D_FF), lambda b, t: (0, 0)),
            _resident((D_FF, D_MODEL), lambda b, t: (0, 0)),
        ],
        out_specs=pl.BlockSpec((None, TM, D_MODEL), lambda b, t: (b, t, 0)),
        out_shape=jax.ShapeDtypeStruct((bsz, n_tiles * TM, D_MODEL), F32),
        compiler_params=_cparams(("arbitrary", "arbitrary")),
        name="ffn",
    )(xa, modsel, nw, w_in, w_out)


def _inproj_kernel(x_ref, mod_ref, nw_ref, w_ref, cr_ref, sr_ref, cd_ref, sa_ref, sb_ref,
                   qr_ref, kr_ref, vr_ref, rg_ref, qd_ref, kd_ref, vd_ref, gt_ref):
    x = x_ref[...]
    h = _rms(x, nw_ref[2:3, :]) * (1.0 + mod_ref[4:5, :]) + mod_ref[3:4, :]
    hb = h.astype(BF16)
    cr, sr = cr_ref[...], sr_ref[...]
    cd, sa, sb = cd_ref[...], sa_ref[...], sb_ref[...]

    rq = _mm(hb, w_ref[:, OFF_RQ:OFF_RK])
    rk = _mm(hb, w_ref[:, OFF_RK:OFF_RV])
    for hd in range(H_RET):
        sl = slice(hd * RET_DK, (hd + 1) * RET_DK)
        q = rq[:, sl]
        k = rk[:, sl]
        qr_ref[:, sl] = (q * cr + pltpu.roll(q, 64, 1) * sr).astype(BF16)
        kr_ref[:, sl] = ((k * cr + pltpu.roll(k, 64, 1) * sr) * (RET_DK ** -0.5)).astype(BF16)
    vr_ref[...] = _mm(hb, w_ref[:, OFF_RV:OFF_RG]).astype(BF16)
    rg_ref[...] = _mm(hb, w_ref[:, OFF_RG:OFF_DQ]).astype(BF16)

    dq = _mm(hb, w_ref[:, OFF_DQ:OFF_DK])
    dk = _mm(hb, w_ref[:, OFF_DK:OFF_DV])
    dv = _mm(hb, w_ref[:, OFF_DV:OFF_GT])
    for hd in range(H_DIFF):
        sl = slice(hd * DIFF_DV, (hd + 1) * DIFF_DV)
        q = dq[:, sl]
        k = dk[:, sl]
        qrot = q * cd + pltpu.roll(q, 16, 1) * sa + pltpu.roll(q, 112, 1) * sb
        krot = k * cd + pltpu.roll(k, 16, 1) * sa + pltpu.roll(k, 112, 1) * sb
        qd_ref[hd] = (qrot * (DIFF_DH ** -0.5 * LOG2E)).astype(BF16)
        kd_ref[hd] = krot.astype(BF16)
        vd_ref[hd, 0:DIFF_DV, :] = dv[:, sl].T.astype(BF16)
        vd_ref[hd, DIFF_DV:ATT_VROWS, :] = jnp.ones((ATT_VROWS - DIFF_DV, TM), BF16)
    gt_ref[...] = _mm(hb, w_ref[:, OFF_GT:IN_WIDTH]).astype(BF16)


def _inproj(xa, modsel, nw, w_in, tables, *, ctx_tile):
    bsz, n, _ = xa.shape
    n_tiles = n // TM
    tok = lambda width: pl.BlockSpec((None, TM, width), lambda b, t: (b, t, 0))
    tab = pl.BlockSpec((TM, 128), lambda b, t: (t, 0))
    heads = pl.BlockSpec((None, H_DIFF, TM, DIFF_DV), lambda b, t: (b, 0, t, 0))
    tiles_per_chunk = ATT_TK // TM
    heads_t = pl.BlockSpec((None, H_DIFF, None, ATT_VROWS, TM),
                           lambda b, t: (b, 0, t // tiles_per_chunk, 0, t % tiles_per_chunk))
    sds = jax.ShapeDtypeStruct
    return pl.pallas_call(
        _inproj_kernel,
        grid=(bsz, n_tiles),
        in_specs=[
            tok(D_MODEL),
            pl.BlockSpec((None, None, N_MOD, D_MODEL), lambda b, t: (b, t // ctx_tile, 0, 0)),
            _resident((6, D_MODEL), lambda b, t: (0, 0)),
            _resident((D_MODEL, IN_WIDTH), lambda b, t: (0, 0)),
            tab, tab, tab, tab, tab,
        ],
        out_specs=[tok(512), tok(512), tok(1024), tok(1024), heads, heads, heads_t, tok(2048)],
        out_shape=[
            sds((bsz, n, 512), BF16), sds((bsz, n, 512), BF16),
            sds((bsz, n, 1024), BF16), sds((bsz, n, 1024), BF16),
            sds((bsz, H_DIFF, n, DIFF_DV), BF16), sds((bsz, H_DIFF, n, DIFF_DV), BF16),
            sds((bsz, H_DIFF, n // ATT_TK, ATT_VROWS, ATT_TK), BF16), sds((bsz, n, 2048), BF16),
        ],
        compiler_params=_cparams(("arbitrary", "arbitrary")),
        name="inproj",
    )(xa, modsel, nw, w_in, *tables)


def _retention_kernel(dl_ref, qf_ref, kf_ref, vf_ref, qb_ref, kb_ref, vb_ref, yf_ref, yb_ref,
                      st_ref, dm_ref, zt_ref, xi_ref):
    c = RET_BLOCK
    dl = dl_ref[...]
    lg = jnp.minimum(dl, 0.0) - jnp.log(1.0 + jnp.exp(-jnp.abs(dl)))

    @pl.when(pl.program_id(0) == 0)
    def _init():
        st_ref[...] = jnp.zeros_like(st_ref)
        ri = lax.broadcasted_iota(jnp.int32, (c, c), 0).astype(F32)
        ci = lax.broadcasted_iota(jnp.int32, (c, c), 1).astype(F32)
        pos = lax.broadcasted_iota(jnp.int32, (c, RET_DK), 0).astype(F32)
        for d in range(2):
            diff = ri - ci if d == 0 else ci - ri
            for hd in range(H_RET):
                g = lg[d:d + 1, hd:hd + 1]
                dm_ref[d, hd] = jnp.where(diff >= 0.0, jnp.exp(jnp.maximum(diff, 0.0) * g), 0.0)
                if d == 0:
                    zt_ref[d, hd] = jnp.exp((c - 1.0 - pos) * g)
                    xi_ref[d, hd] = jnp.exp((pos + 1.0) * g)
                else:
                    zt_ref[d, hd] = jnp.exp(pos * g)
                    xi_ref[d, hd] = jnp.exp((c - pos) * g)

    for d, (q_ref, k_ref, v_ref, y_ref) in enumerate(
            ((qf_ref, kf_ref, vf_ref, yf_ref), (qb_ref, kb_ref, vb_ref, yb_ref))):
        for hd in range(H_RET):
            g_blk = jnp.exp(float(c) * lg[d:d + 1, hd:hd + 1])
            for b in range(q_ref.shape[0]):
                q = q_ref[b, :, hd * RET_DK:(hd + 1) * RET_DK]
                k = k_ref[b, :, hd * RET_DK:(hd + 1) * RET_DK]
                v = v_ref[b, :, hd * RET_DV:(hd + 1) * RET_DV]
                s = _mm_nt(q, k) * dm_ref[d, hd]
                inner = _mm(s.astype(BF16), v)
                state = st_ref[b, d, hd]
                qx = (q.astype(F32) * xi_ref[d, hd]).astype(BF16)
                cross = _mm(qx, state.astype(BF16))
                y_ref[b, :, hd * RET_DV:(hd + 1) * RET_DV] = (inner + cross).astype(BF16)
                kz = (k.astype(F32) * zt_ref[d, hd]).astype(BF16)
                st_ref[b, d, hd] = g_blk * state + _mm_tn(kz, v)


def _retention(decay_logit, q, k, v):
    bsz, n, _ = q.shape
    nb = n // RET_BLOCK
    last = nb - 1
    fwd = lambda i: (0, jnp.where(i == 0, last, i - 1), 0)
    bwd = lambda i: (0, jnp.where(i == 0, last, last - i), 0)
    blk = lambda width, im: pl.BlockSpec((bsz, RET_BLOCK, width), im)
    y = jax.ShapeDtypeStruct((bsz, n, H_RET * RET_DV), BF16)
    return pl.pallas_call(
        _retention_kernel,
        grid=(nb,),
        in_specs=[
            pl.BlockSpec((2, H_RET), lambda i: (0, 0)),
            blk(512, fwd), blk(512, fwd), blk(1024, fwd),
            blk(512, bwd), blk(512, bwd), blk(1024, bwd),
        ],
        out_specs=[blk(1024, fwd), blk(1024, bwd)],
        out_shape=[y, y],
        scratch_shapes=[
            pltpu.VMEM((bsz, 2, H_RET, RET_DK, RET_DV), F32),
            pltpu.VMEM((2, H_RET, RET_BLOCK, RET_BLOCK), F32),
            pltpu.VMEM((2, H_RET, RET_BLOCK, RET_DK), F32),
            pltpu.VMEM((2, H_RET, RET_BLOCK, RET_DK), F32),
        ],
        compiler_params=_cparams(("arbitrary",)),
        name="retention",
    )(decay_logit, q, k, v, q, k, v)


def _attn_kernel(lam_ref, sw_ref, q_ref, k_ref, vt_ref, o_ref, qq_ref, m_ref, acc_ref, s_ref, cm_ref,
                 *, tq, tk, n_chunks, lam_init):
    qt = q_ref[...].astype(F32).T
    row = lax.broadcasted_iota(jnp.int32, qt.shape, 0)
    qq_ref[:, :tq] = jnp.where(row < DIFF_DH, qt, 0.0).astype(BF16)
    qq_ref[:, tq:] = jnp.where(row >= DIFF_DH, qt, 0.0).astype(BF16)
    m_ref[...] = jnp.full_like(m_ref, -jnp.inf)
    acc_ref[...] = jnp.zeros_like(acc_ref)

    def scores(ci, slot):
        start = pl.multiple_of(ci * tk, tk)
        s = _mm(k_ref[pl.ds(start, tk), :], qq_ref[...])
        s_ref[slot] = s
        cm_ref[slot] = jnp.max(s, axis=0, keepdims=True)

    def softmax_pv(ci, slot):
        s = s_ref[slot]
        m_old = m_ref[...]
        m_new = jnp.maximum(m_old, cm_ref[slot])
        alpha = jnp.exp2(m_old - m_new)
        p = jnp.exp2(s - m_new)
        acc_ref[...] = alpha * acc_ref[...] + _mm(vt_ref[ci], p.astype(BF16))
        m_ref[...] = m_new

    scores(0, 0)
    n_loop = (n_chunks - 1) // ATT_UNROLL

    n_rb = tk // ATT_RB

    def fused(c_next, c_cur, slot):
        nstart = pl.multiple_of(c_next * tk, tk)
        m_old = m_ref[...]
        m_new = jnp.maximum(m_old, cm_ref[slot])
        alpha = jnp.exp2(m_old - m_new)
        pv = None
        cmax = None
        for r in range(n_rb):
            rows = slice(r * ATT_RB, (r + 1) * ATT_RB)
            sn = _mm(k_ref[pl.ds(nstart + r * ATT_RB, ATT_RB), :], qq_ref[...])
            s_ref[1 - slot, rows, :] = sn
            cm_r = jnp.max(sn, axis=0, keepdims=True)
            cmax = cm_r if cmax is None else jnp.maximum(cmax, cm_r)
            p = jnp.exp2(s_ref[slot, rows, :] - m_new)
            pv_r = _mm(vt_ref[c_cur, :, rows], p.astype(BF16))
            pv = pv_r if pv is None else pv + pv_r
        cm_ref[1 - slot] = cmax
        acc_ref[...] = alpha * acc_ref[...] + pv
        m_ref[...] = m_new

    def body(j, carry):
        c0 = ATT_UNROLL * j
        for u in range(ATT_UNROLL):
            fused(c0 + u + 1, c0 + u, u % 2)
        return carry

    if n_loop > 0:
        lax.fori_loop(0, n_loop, body, 0)
    for c in range(n_loop * ATT_UNROLL, n_chunks):
        if c + 1 < n_chunks:
            scores(c + 1, (c + 1) % 2)
        softmax_pv(c, c % 2)

    lv = lam_ref[...]
    lam = (jnp.exp(jnp.sum(lv[0:1, :] * lv[1:2, :], axis=1, keepdims=True))
           - jnp.exp(jnp.sum(lv[2:3, :] * lv[3:4, :], axis=1, keepdims=True)) + lam_init)
    acc = acc_ref[0:DIFF_DV, :]
    l = acc_ref[DIFF_DV:DIFF_DV + 1, :]
    o = acc[:, :tq] / l[:, :tq] - lam * (acc[:, tq:] / l[:, tq:])
    o = o * lax.rsqrt(jnp.mean(o * o, axis=0, keepdims=True) + EPS) * sw_ref[...] * (1.0 - lam_init)
    o_ref[...] = o.T.astype(BF16)


def _attention(lam_vec, subln_w, q, k, vt, *, tq, tk, q_block0, n_q, kv_block0, n_kv, lam_init):
    bsz = q.shape[0]
    n_chunks = n_kv // tk
    if tk == ATT_TK:
        assert kv_block0 == 0 and n_chunks == vt.shape[2]
        vt_spec = pl.BlockSpec((None, None, n_chunks, ATT_VROWS, tk), lambda b, h, i: (b, h, 0, 0, 0))
    else:
        assert n_chunks == 1 and (kv_block0 + 1) * tk == vt.shape[2] * ATT_TK
        vt_spec = pl.BlockSpec((None, None, 1, ATT_VROWS, tk),
                               lambda b, h, i: (b, h, vt.shape[2] - 1, 0, ATT_TK // tk - 1))
    kern = functools.partial(_attn_kernel, tq=tq, tk=tk, n_chunks=n_chunks, lam_init=lam_init)
    return pl.pallas_call(
        kern,
        grid=(bsz, H_DIFF, n_q),
        in_specs=[
            pl.BlockSpec((4, DIFF_DH), lambda b, h, i: (0, 0)),
            pl.BlockSpec((DIFF_DV, 1), lambda b, h, i: (0, 0)),
            pl.BlockSpec((None, None, tq, DIFF_DV), lambda b, h, i: (b, h, q_block0 + i, 0)),
            pl.BlockSpec((None, None, n_kv, DIFF_DV), lambda b, h, i: (b, h, kv_block0, 0)),
            vt_spec,
        ],
        out_specs=pl.BlockSpec((None, tq, DIFF_DV), lambda b, h, i: (b, i, h)),
        out_shape=jax.ShapeDtypeStruct((bsz, n_q * tq, H_DIFF * DIFF_DV), BF16),
        scratch_shapes=[
            pltpu.VMEM((DIFF_DV, 2 * tq), BF16),
            pltpu.VMEM((1, 2 * tq), F32),
            pltpu.VMEM((ATT_VROWS, 2 * tq), F32),
            pltpu.VMEM((2, tk, 2 * tq), F32),
            pltpu.VMEM((2, 1, 2 * tq), F32),
        ],
        compiler_params=_cparams(("arbitrary", "arbitrary", "arbitrary")),
        name="diffattn",
    )(lam_vec, subln_w.reshape(DIFF_DV, 1), q, k, vt)


def _merge_kernel(x_ref, mod_ref, nw_ref, gn_ref, yf_ref, yb_ref, rg_ref, al_ref, ac_ref, gt_ref, w_ref, o_ref,
                  *, ctx_tile):
    x = x_ref[...]
    is_ctx = pl.program_id(1) == ctx_tile
    diff = jnp.where(is_ctx, ac_ref[...], al_ref[...]).astype(F32)
    y = yf_ref[...].astype(F32) + yb_ref[...].astype(F32)
    rg = rg_ref[...].astype(F32)
    parts = []
    for hd in range(H_RET):
        sl = slice(hd * RET_DV, (hd + 1) * RET_DV)
        yh = y[:, sl]
        mu = jnp.mean(yh, axis=-1, keepdims=True)
        yc = yh - mu
        var = jnp.mean(yc * yc, axis=-1, keepdims=True)
        parts.append(yc * lax.rsqrt(var + EPS) * gn_ref[:, sl])
    ret = jnp.concatenate(parts, axis=1) * (rg * jax.nn.sigmoid(rg))
    ga = jax.nn.sigmoid(gt_ref[:, :D_MODEL].astype(F32))
    gb = jax.nn.sigmoid(gt_ref[:, D_MODEL:].astype(F32))
    z = (ga * ret + gb * diff).astype(BF16)
    yo = _mm(z, w_ref[...])
    o_ref[...] = x + mod_ref[5:6, :] * _rms(yo, nw_ref[3:4, :])


def _merge(xa, modsel, nw, gn_w, yf, yb, rg, at_lat, at_ctx, gates, w_out, *, ctx_tile):
    bsz, n, _ = xa.shape
    n_tiles = n // TM
    tok = lambda width: pl.BlockSpec((None, TM, width), lambda b, t: (b, t, 0))
    kern = functools.partial(_merge_kernel, ctx_tile=ctx_tile)
    return pl.pallas_call(
        kern,
        grid=(bsz, n_tiles),
        in_specs=[
            tok(D_MODEL),
            pl.BlockSpec((None, None, N_MOD, D_MODEL), lambda b, t: (b, t // ctx_tile, 0, 0)),
            _resident((6, D_MODEL), lambda b, t: (0, 0)),
            _resident((1, D_MODEL), lambda b, t: (0, 0)),
            tok(1024), tok(1024), tok(1024),
            pl.BlockSpec((None, TM, D_MODEL), lambda b, t: (b, jnp.minimum(t, ctx_tile - 1), 0)),
            pl.BlockSpec((None, TM, D_MODEL), lambda b, t: (b, 0, 0)),
            tok(2048),
            _resident((D_MODEL, D_MODEL), lambda b, t: (0, 0)),
        ],
        out_specs=tok(D_MODEL),
        out_shape=jax.ShapeDtypeStruct((bsz, n, D_MODEL), F32),
        compiler_params=_cparams(("arbitrary", "arbitrary")),
        name="merge",
    )(xa, modsel, nw, gn_w, yf, yb, rg, at_lat, at_ctx, gates, w_out)


def _rope_tables(t, n):
    pos = jnp.arange(n, dtype=jnp.int32)
    is_lat = (pos < t)[:, None]
    lane = jnp.arange(128)
    inv_r = ROPE_BASE ** (-jnp.arange(64, dtype=F32) / 64)
    ang = pos.astype(F32)[:, None] * inv_r[None, :]
    cos, sin = jnp.cos(ang), jnp.sin(ang)
    cr = jnp.where(is_lat, jnp.concatenate([cos, cos], axis=1), 1.0)
    sr = jnp.where(is_lat, jnp.concatenate([-sin, sin], axis=1), 0.0)
    inv_d = ROPE_BASE ** (-jnp.arange(16, dtype=F32) / 16)
    row = (pos // GRID_W).astype(F32)[:, None] * inv_d[None, :]
    col = (pos % GRID_W).astype(F32)[:, None] * inv_d[None, :]
    cos64 = jnp.concatenate([jnp.cos(row)] * 2 + [jnp.cos(col)] * 2, axis=1)
    sin64 = jnp.concatenate([jnp.sin(row)] * 2 + [jnp.sin(col)] * 2, axis=1)
    cos128 = jnp.concatenate([cos64, cos64], axis=1)
    sin128 = jnp.concatenate([sin64, sin64], axis=1)
    upper = ((lane % 32) >= 16)[None, :]
    cd = jnp.where(is_lat, cos128, 1.0)
    sa = jnp.where(is_lat & upper, sin128, 0.0)
    sb = jnp.where(is_lat & ~upper, -sin128, 0.0)
    return cr, sr, cd, sa, sb


def kernel(x, c, ctx, c_ctx, w_ada, b_ada, norm_w, ffn_w_in, ffn_w_out, w_in, w_out,
           ret_decay_logit, ret_gn_w, diff_lambda, diff_subln_w):
    bsz, t, d = x.shape
    n_ctx = ctx.shape[1]
    depth = w_ada.shape[0]
    assert d == D_MODEL and n_ctx == TM and t % ATT_TQ == 0 and bsz + 1 <= 8
    n = t + n_ctx
    assert n % ATT_TK == 0
    n_tiles = n // TM
    ctx_tile = n_tiles - 1

    xa = jnp.concatenate([x, ctx], axis=1)
    cs = jnp.zeros((8, d), F32).at[:bsz].set(c).at[bsz].set(c_ctx)
    mod = _adaln(cs, w_ada, b_ada).reshape(depth, 8, N_MOD, d)
    modsel = jnp.stack([mod[:, :bsz], jnp.broadcast_to(mod[:, bsz:bsz + 1], (depth, bsz, N_MOD, d))], axis=2)
    tables = _rope_tables(t, n)

    ffn_w_in_b = ffn_w_in.astype(BF16)
    ffn_w_out_b = ffn_w_out.astype(BF16)
    w_in_b = w_in.astype(BF16)
    w_out_b = w_out.astype(BF16)

    out = None
    for l in range(depth):
        last = l == depth - 1
        lam_init = 0.8 - 0.6 * math.exp(-0.3 * l)
        nw = norm_w[l]
        xa = _ffn(xa, modsel[l], nw, ffn_w_in_b[l, 0], ffn_w_out_b[l, 0],
                  mod_off=0, nw_off=0, n_tiles=n_tiles, ctx_tile=ctx_tile)
        qr, kr, vr, rg, qd, kd, vd, gates = _inproj(xa, modsel[l], nw, w_in_b[l], tables, ctx_tile=ctx_tile)
        yf, yb = _retention(ret_decay_logit[l], qr, kr, vr)
        at_lat = _attention(diff_lambda[l], diff_subln_w[l], qd, kd, vd, tq=ATT_TQ, tk=ATT_TK,
                            q_block0=0, n_q=t // ATT_TQ, kv_block0=0, n_kv=n, lam_init=lam_init)
        at_ctx = _attention(diff_lambda[l], diff_subln_w[l], qd, kd, vd, tq=TM, tk=TM,
                            q_block0=ctx_tile, n_q=1, kv_block0=ctx_tile, n_kv=TM, lam_init=lam_init)
        xa = _merge(xa, modsel[l], nw, ret_gn_w[l].reshape(1, d), yf, yb, rg, at_lat, at_ctx, gates,
                    w_out_b[l], ctx_tile=ctx_tile)
        if last:
            out = _ffn(xa, modsel[l], nw, ffn_w_in_b[l, 1], ffn_w_out_b[l, 1],
                       mod_off=6, nw_off=4, n_tiles=t // TM, ctx_tile=ctx_tile)
        else:
            xa = _ffn(xa, modsel[l], nw, ffn_w_in_b[l, 1], ffn_w_out_b[l, 1],
                      mod_off=6, nw_off=4, n_tiles=n_tiles, ctx_tile=ctx_tile)
    return out
```

```python
import functools
import math

import jax
import jax.numpy as jnp
from jax import lax
from jax.experimental import pallas as pl
from jax.experimental.pallas import tpu as pltpu

D_MODEL = 1024
D_FF = 2816
H_RET = 4
RET_DK = 128
RET_DV = 256
H_DIFF = 8
DIFF_DV = 128
DIFF_DH = 64
GRID_W = 64
ROPE_BASE = 10000.0
EPS = 1e-6
LOG2E = 1.4426950408889634
N_MOD = 9
IN_WIDTH = 8192
OFF_RQ, OFF_RK, OFF_RV, OFF_RG, OFF_DQ, OFF_DK, OFF_DV, OFF_GT = 0, 512, 1024, 2048, 3072, 4096, 5120, 6144

TM = 256
RET_BLOCK = 256
ATT_TQ = 512
ATT_TK = 1280
ATT_UNROLL = 4
ATT_RB = 256
ATT_VROWS = 144
VMEM_LIMIT = 56 * 1024 * 1024

BF16 = jnp.bfloat16
F32 = jnp.float32


def _cparams(sem):
    return pltpu.CompilerParams(dimension_semantics=sem, vmem_limit_bytes=VMEM_LIMIT)


def _resident(shape, index_map):
    return pl.BlockSpec(shape, index_map, pipeline_mode=pl.Buffered(1))


def _rms(x, g):
    return x * lax.rsqrt(jnp.mean(x * x, axis=-1, keepdims=True) + EPS) * g


def _mm(a, b):
    return jnp.dot(a, b, preferred_element_type=F32)


def _mm_nt(a, b):
    return lax.dot_general(a, b, (((1,), (1,)), ((), ())), preferred_element_type=F32)


def _mm_tn(a, b):
    return lax.dot_general(a, b, (((0,), (0,)), ((), ())), preferred_element_type=F32)


def _adaln_kernel(c_ref, w_ref, b_ref, o_ref):
    c = c_ref[...]
    s = c * jax.nn.sigmoid(c)
    o_ref[...] = _mm(s, w_ref[...]) + b_ref[...]


def _adaln(cs, w_ada, b_ada):
    depth = w_ada.shape[0]
    width = w_ada.shape[2]
    tn = 1024
    return pl.pallas_call(
        _adaln_kernel,
        grid=(depth, width // tn),
        in_specs=[
            pl.BlockSpec((8, D_MODEL), lambda l, j: (0, 0)),
            pl.BlockSpec((None, D_MODEL, tn), lambda l, j: (l, 0, j)),
            pl.BlockSpec((None, 1, tn), lambda l, j: (l, 0, j)),
        ],
        out_specs=pl.BlockSpec((None, 8, tn), lambda l, j: (l, 0, j)),
        out_shape=jax.ShapeDtypeStruct((depth, 8, width), F32),
        compiler_params=_cparams(("arbitrary", "arbitrary")),
        name="adaln",
    )(cs, w_ada, b_ada.reshape(depth, 1, width))


def _ffn_kernel(x_ref, mod_ref, nw_ref, win_ref, wout_ref, o_ref, *, mod_off, nw_off):
    x = x_ref[...]
    shift = mod_ref[mod_off:mod_off + 1, :]
    scale = mod_ref[mod_off + 1:mod_off + 2, :]
    gate = mod_ref[mod_off + 2:mod_off + 3, :]
    h = _rms(x, nw_ref[nw_off:nw_off + 1, :]) * (1.0 + scale) + shift
    hb = h.astype(BF16)
    a = _mm(hb, win_ref[:, :D_FF])
    b = _mm(hb, win_ref[:, D_FF:])
    u = (a * jax.nn.sigmoid(a) * b).astype(BF16)
    y = _mm(u, wout_ref[...])
    o_ref[...] = x + 0.5 * gate * _rms(y, nw_ref[nw_off + 1:nw_off + 2, :])


def _ffn(xa, modsel, nw, w_in, w_out, *, mod_off, nw_off, n_tiles, ctx_tile):
    bsz = xa.shape[0]
    kern = functools.partial(_ffn_kernel, mod_off=mod_off, nw_off=nw_off)
    return pl.pallas_call(
        kern,
        grid=(bsz, n_tiles),
        in_specs=[
            pl.BlockSpec((None, TM, D_MODEL), lambda b, t: (b, t, 0)),
            pl.BlockSpec((None, None, N_MOD, D_MODEL), lambda b, t: (b, t // ctx_tile, 0, 0)),
            _resident((6, D_MODEL), lambda b, t: (0, 0)),
            _resident((D_MODEL, 2 * D_FF), lambda b, t: (0, 0)),
            _resident((D_FF, D_MODEL), lambda b, t: (0, 0)),
        ],
        out_specs=pl.BlockSpec((None, TM, D_MODEL), lambda b, t: (b, t, 0)),
        out_shape=jax.ShapeDtypeStruct((bsz, n_tiles * TM, D_MODEL), F32),
        compiler_params=_cparams(("arbitrary", "arbitrary")),
        name="ffn",
    )(xa, modsel, nw, w_in, w_out)


def _inproj_kernel(x_ref, mod_ref, nw_ref, w_ref, cr_ref, sr_ref, cd_ref, sa_ref, sb_ref,
                   qr_ref, kr_ref, vr_ref, rg_ref, qd_ref, kd_ref, vd_ref, gt_ref):
    x = x_ref[...]
    h = _rms(x, nw_ref[2:3, :]) * (1.0 + mod_ref[4:5, :]) + mod_ref[3:4, :]
    hb = h.astype(BF16)
    cr, sr = cr_ref[...], sr_ref[...]
    cd, sa, sb = cd_ref[...], sa_ref[...], sb_ref[...]

    rq = _mm(hb, w_ref[:, OFF_RQ:OFF_RK])
    rk = _mm(hb, w_ref[:, OFF_RK:OFF_RV])
    for hd in range(H_RET):
        sl = slice(hd * RET_DK, (hd + 1) * RET_DK)
        q = rq[:, sl]
        k = rk[:, sl]
        qr_ref[:, sl] = (q * cr + pltpu.roll(q, 64, 1) * sr).astype(BF16)
        kr_ref[:, sl] = ((k * cr + pltpu.roll(k, 64, 1) * sr) * (RET_DK ** -0.5)).astype(BF16)
    vr_ref[...] = _mm(hb, w_ref[:, OFF_RV:OFF_RG]).astype(BF16)
    rg = _mm(hb, w_ref[:, OFF_RG:OFF_DQ])
    rg_ref[...] = (rg * jax.nn.sigmoid(rg)).astype(BF16)

    dq = _mm(hb, w_ref[:, OFF_DQ:OFF_DK])
    dk = _mm(hb, w_ref[:, OFF_DK:OFF_DV])
    dv = _mm(hb, w_ref[:, OFF_DV:OFF_GT])
    for hd in range(H_DIFF):
        sl = slice(hd * DIFF_DV, (hd + 1) * DIFF_DV)
        q = dq[:, sl]
        k = dk[:, sl]
        qrot = q * cd + pltpu.roll(q, 16, 1) * sa + pltpu.roll(q, 112, 1) * sb
        krot = k * cd + pltpu.roll(k, 16, 1) * sa + pltpu.roll(k, 112, 1) * sb
        qd_ref[hd] = (qrot * (DIFF_DH ** -0.5 * LOG2E)).astype(BF16)
        kd_ref[hd] = krot.astype(BF16)
        vd_ref[hd, 0:DIFF_DV, :] = dv[:, sl].T.astype(BF16)
        vd_ref[hd, DIFF_DV:ATT_VROWS, :] = jnp.ones((ATT_VROWS - DIFF_DV, TM), BF16)
    gt_ref[...] = jax.nn.sigmoid(_mm(hb, w_ref[:, OFF_GT:IN_WIDTH])).astype(BF16)


def _inproj(xa, modsel, nw, w_in, tables, *, ctx_tile):
    bsz, n, _ = xa.shape
    n_tiles = n // TM
    tok = lambda width: pl.BlockSpec((None, TM, width), lambda b, t: (b, t, 0))
    tab = pl.BlockSpec((TM, 128), lambda b, t: (t, 0))
    heads = pl.BlockSpec((None, H_DIFF, TM, DIFF_DV), lambda b, t: (b, 0, t, 0))
    tiles_per_chunk = ATT_TK // TM
    heads_t = pl.BlockSpec((None, H_DIFF, None, ATT_VROWS, TM),
                           lambda b, t: (b, 0, t // tiles_per_chunk, 0, t % tiles_per_chunk))
    sds = jax.ShapeDtypeStruct
    return pl.pallas_call(
        _inproj_kernel,
        grid=(bsz, n_tiles),
        in_specs=[
            tok(D_MODEL),
            pl.BlockSpec((None, None, N_MOD, D_MODEL), lambda b, t: (b, t // ctx_tile, 0, 0)),
            _resident((6, D_MODEL), lambda b, t: (0, 0)),
            _resident((D_MODEL, IN_WIDTH), lambda b, t: (0, 0)),
            tab, tab, tab, tab, tab,
        ],
        out_specs=[tok(512), tok(512), tok(1024), tok(1024), heads, heads, heads_t, tok(2048)],
        out_shape=[
            sds((bsz, n, 512), BF16), sds((bsz, n, 512), BF16),
            sds((bsz, n, 1024), BF16), sds((bsz, n, 1024), BF16),
            sds((bsz, H_DIFF, n, DIFF_DV), BF16), sds((bsz, H_DIFF, n, DIFF_DV), BF16),
            sds((bsz, H_DIFF, n // ATT_TK, ATT_VROWS, ATT_TK), BF16), sds((bsz, n, 2048), BF16),
        ],
        compiler_params=_cparams(("arbitrary", "arbitrary")),
        name="inproj",
    )(xa, modsel, nw, w_in, *tables)


def _retention_kernel(dl_ref, qf_ref, kf_ref, vf_ref, qb_ref, kb_ref, vb_ref, yf_ref, yb_ref,
                      st_ref, dm_ref, zt_ref, xi_ref):
    c = RET_BLOCK
    dl = dl_ref[...]
    lg = jnp.minimum(dl, 0.0) - jnp.log(1.0 + jnp.exp(-jnp.abs(dl)))

    @pl.when(pl.program_id(0) == 0)
    def _init():
        st_ref[...] = jnp.zeros_like(st_ref)
        ri = lax.broadcasted_iota(jnp.int32, (c, c), 0).astype(F32)
        ci = lax.broadcasted_iota(jnp.int32, (c, c), 1).astype(F32)
        pos = lax.broadcasted_iota(jnp.int32, (c, RET_DK), 0).astype(F32)
        for d in range(2):
            diff = ri - ci if d == 0 else ci - ri
            for hd in range(H_RET):
                g = lg[d:d + 1, hd:hd + 1]
                dm_ref[d, hd] = jnp.where(diff >= 0.0, jnp.exp(jnp.maximum(diff, 0.0) * g), 0.0)
                if d == 0:
                    zt_ref[d, hd] = jnp.exp((c - 1.0 - pos) * g)
                    xi_ref[d, hd] = jnp.exp((pos + 1.0) * g)
                else:
                    zt_ref[d, hd] = jnp.exp(pos * g)
                    xi_ref[d, hd] = jnp.exp((c - pos) * g)

    for d, (q_ref, k_ref, v_ref, y_ref) in enumerate(
            ((qf_ref, kf_ref, vf_ref, yf_ref), (qb_ref, kb_ref, vb_ref, yb_ref))):
        for hd in range(H_RET):
            g_blk = jnp.exp(float(c) * lg[d:d + 1, hd:hd + 1])
            for b in range(q_ref.shape[0]):
                q = q_ref[b, :, hd * RET_DK:(hd + 1) * RET_DK]
                k = k_ref[b, :, hd * RET_DK:(hd + 1) * RET_DK]
                v = v_ref[b, :, hd * RET_DV:(hd + 1) * RET_DV]
                s = _mm_nt(q, k) * dm_ref[d, hd]
                inner = _mm(s.astype(BF16), v)
                state = st_ref[b, d, hd]
                qx = (q.astype(F32) * xi_ref[d, hd]).astype(BF16)
                cross = _mm(qx, state.astype(BF16))
                y_ref[b, :, hd * RET_DV:(hd + 1) * RET_DV] = (inner + cross).astype(BF16)
                kz = (k.astype(F32) * zt_ref[d, hd]).astype(BF16)
                st_ref[b, d, hd] = g_blk * state + _mm_tn(kz, v)


def _retention(decay_logit, q, k, v):
    bsz, n, _ = q.shape
    nb = n // RET_BLOCK
    last = nb - 1
    fwd = lambda i: (0, jnp.where(i == 0, last, i - 1), 0)
    bwd = lambda i: (0, jnp.where(i == 0, last, last - i), 0)
    blk = lambda width, im: pl.BlockSpec((bsz, RET_BLOCK, width), im)
    y = jax.ShapeDtypeStruct((bsz, n, H_RET * RET_DV), BF16)
    return pl.pallas_call(
        _retention_kernel,
        grid=(nb,),
        in_specs=[
            pl.BlockSpec((2, H_RET), lambda i: (0, 0)),
            blk(512, fwd), blk(512, fwd), blk(1024, fwd),
            blk(512, bwd), blk(512, bwd), blk(1024, bwd),
        ],
        out_specs=[blk(1024, fwd), blk(1024, bwd)],
        out_shape=[y, y],
        scratch_shapes=[
            pltpu.VMEM((bsz, 2, H_RET, RET_DK, RET_DV), F32),
            pltpu.VMEM((2, H_RET, RET_BLOCK, RET_BLOCK), F32),
            pltpu.VMEM((2, H_RET, RET_BLOCK, RET_DK), F32),
            pltpu.VMEM((2, H_RET, RET_BLOCK, RET_DK), F32),
        ],
        compiler_params=_cparams(("arbitrary",)),
        name="retention",
    )(decay_logit, q, k, v, q, k, v)


def _attn_kernel(lam_ref, sw_ref, q_ref, k_ref, vt_ref, o_ref, qq_ref, m_ref, acc_ref, s_ref, cm_ref,
                 *, tq, tk, n_chunks, lam_init):
    qt = q_ref[...].astype(F32).T
    row = lax.broadcasted_iota(jnp.int32, qt.shape, 0)
    qq_ref[:, :tq] = jnp.where(row < DIFF_DH, qt, 0.0).astype(BF16)
    qq_ref[:, tq:] = jnp.where(row >= DIFF_DH, qt, 0.0).astype(BF16)
    m_ref[...] = jnp.full_like(m_ref, -jnp.inf)
    acc_ref[...] = jnp.zeros_like(acc_ref)

    def scores(ci, slot):
        start = pl.multiple_of(ci * tk, tk)
        s = _mm(k_ref[pl.ds(start, tk), :], qq_ref[...])
        s_ref[slot] = s
        cm_ref[slot] = jnp.max(s, axis=0, keepdims=True)

    def softmax_pv(ci, slot):
        s = s_ref[slot]
        m_old = m_ref[...]
        m_new = jnp.maximum(m_old, cm_ref[slot])
        alpha = jnp.exp2(m_old - m_new)
        p = jnp.exp2(s - m_new)
        acc_ref[...] = alpha * acc_ref[...] + _mm(vt_ref[ci], p.astype(BF16))
        m_ref[...] = m_new

    scores(0, 0)
    n_loop = (n_chunks - 1) // ATT_UNROLL

    n_rb = tk // ATT_RB

    def fused(c_next, c_cur, slot):
        nstart = pl.multiple_of(c_next * tk, tk)
        m_old = m_ref[...]
        m_new = jnp.maximum(m_old, cm_ref[slot])
        alpha = jnp.exp2(m_old - m_new)
        pv = None
        cmax = None
        for r in range(n_rb):
            rows = slice(r * ATT_RB, (r + 1) * ATT_RB)
            sn = _mm(k_ref[pl.ds(nstart + r * ATT_RB, ATT_RB), :], qq_ref[...])
            s_ref[1 - slot, rows, :] = sn
            cm_r = jnp.max(sn, axis=0, keepdims=True)
            cmax = cm_r if cmax is None else jnp.maximum(cmax, cm_r)
            p = jnp.exp2(s_ref[slot, rows, :] - m_new)
            pv_r = _mm(vt_ref[c_cur, :, rows], p.astype(BF16))
            pv = pv_r if pv is None else pv + pv_r
        cm_ref[1 - slot] = cmax
        acc_ref[...] = alpha * acc_ref[...] + pv
        m_ref[...] = m_new

    def body(j, carry):
        c0 = ATT_UNROLL * j
        for u in range(ATT_UNROLL):
            fused(c0 + u + 1, c0 + u, u % 2)
        return carry

    if n_loop > 0:
        lax.fori_loop(0, n_loop, body, 0)
    for c in range(n_loop * ATT_UNROLL, n_chunks):
        if c + 1 < n_chunks:
            scores(c + 1, (c + 1) % 2)
        softmax_pv(c, c % 2)

    lv = lam_ref[...]
    lam = (jnp.exp(jnp.sum(lv[0:1, :] * lv[1:2, :], axis=1, keepdims=True))
           - jnp.exp(jnp.sum(lv[2:3, :] * lv[3:4, :], axis=1, keepdims=True)) + lam_init)
    acc = acc_ref[0:DIFF_DV, :]
    l = acc_ref[DIFF_DV:DIFF_DV + 1, :]
    o = acc[:, :tq] / l[:, :tq] - lam * (acc[:, tq:] / l[:, tq:])
    o = o * lax.rsqrt(jnp.mean(o * o, axis=0, keepdims=True) + EPS) * sw_ref[...] * (1.0 - lam_init)
    o_ref[...] = o.T.astype(BF16)


def _attention(lam_vec, subln_w, q, k, vt, *, tq, tk, q_block0, n_q, kv_block0, n_kv, lam_init):
    bsz = q.shape[0]
    n_chunks = n_kv // tk
    if tk == ATT_TK:
        assert kv_block0 == 0 and n_chunks == vt.shape[2]
        vt_spec = pl.BlockSpec((None, None, n_chunks, ATT_VROWS, tk), lambda b, h, i: (b, h, 0, 0, 0))
    else:
        assert n_chunks == 1 and (kv_block0 + 1) * tk == vt.shape[2] * ATT_TK
        vt_spec = pl.BlockSpec((None, None, 1, ATT_VROWS, tk),
                               lambda b, h, i: (b, h, vt.shape[2] - 1, 0, ATT_TK // tk - 1))
    kern = functools.partial(_attn_kernel, tq=tq, tk=tk, n_chunks=n_chunks, lam_init=lam_init)
    return pl.pallas_call(
        kern,
        grid=(bsz, H_DIFF, n_q),
        in_specs=[
            pl.BlockSpec((4, DIFF_DH), lambda b, h, i: (0, 0)),
            pl.BlockSpec((DIFF_DV, 1), lambda b, h, i: (0, 0)),
            pl.BlockSpec((None, None, tq, DIFF_DV), lambda b, h, i: (b, h, q_block0 + i, 0)),
            pl.BlockSpec((None, None, n_kv, DIFF_DV), lambda b, h, i: (b, h, kv_block0, 0)),
            vt_spec,
        ],
        out_specs=pl.BlockSpec((None, tq, DIFF_DV), lambda b, h, i: (b, i, h)),
        out_shape=jax.ShapeDtypeStruct((bsz, n_q * tq, H_DIFF * DIFF_DV), BF16),
        scratch_shapes=[
            pltpu.VMEM((DIFF_DV, 2 * tq), BF16),
            pltpu.VMEM((1, 2 * tq), F32),
            pltpu.VMEM((ATT_VROWS, 2 * tq), F32),
            pltpu.VMEM((2, tk, 2 * tq), F32),
            pltpu.VMEM((2, 1, 2 * tq), F32),
        ],
        compiler_params=_cparams(("arbitrary", "arbitrary", "arbitrary")),
        name="diffattn",
    )(lam_vec, subln_w.reshape(DIFF_DV, 1), q, k, vt)


def _merge_kernel(x_ref, mod_ref, nw_ref, gn_ref, yf_ref, yb_ref, rg_ref, al_ref, ac_ref, gt_ref, w_ref, o_ref,
                  *, ctx_tile):
    x = x_ref[...]
    is_ctx = pl.program_id(1) == ctx_tile
    diff = jnp.where(is_ctx, ac_ref[...], al_ref[...]).astype(F32)
    y = yf_ref[...].astype(F32) + yb_ref[...].astype(F32)
    parts = []
    for hd in range(H_RET):
        sl = slice(hd * RET_DV, (hd + 1) * RET_DV)
        yh = y[:, sl]
        mu = jnp.mean(yh, axis=-1, keepdims=True)
        yc = yh - mu
        var = jnp.mean(yc * yc, axis=-1, keepdims=True)
        parts.append(yc * lax.rsqrt(var + EPS) * gn_ref[:, sl])
    ret = jnp.concatenate(parts, axis=1) * rg_ref[...].astype(F32)
    ga = gt_ref[:, :D_MODEL].astype(F32)
    gb = gt_ref[:, D_MODEL:].astype(F32)
    z = (ga * ret + gb * diff).astype(BF16)
    yo = _mm(z, w_ref[...])
    o_ref[...] = x + mod_ref[5:6, :] * _rms(yo, nw_ref[3:4, :])


def _merge(xa, modsel, nw, gn_w, yf, yb, rg, at_lat, at_ctx, gates, w_out, *, ctx_tile):
    bsz, n, _ = xa.shape
    n_tiles = n // TM
    tok = lambda width: pl.BlockSpec((None, TM, width), lambda b, t: (b, t, 0))
    kern = functools.partial(_merge_kernel, ctx_tile=ctx_tile)
    return pl.pallas_call(
        kern,
        grid=(bsz, n_tiles),
        in_specs=[
            tok(D_MODEL),
            pl.BlockSpec((None, None, N_MOD, D_MODEL), lambda b, t: (b, t // ctx_tile, 0, 0)),
            _resident((6, D_MODEL), lambda b, t: (0, 0)),
            _resident((1, D_MODEL), lambda b, t: (0, 0)),
            tok(1024), tok(1024), tok(1024),
            pl.BlockSpec((None, TM, D_MODEL), lambda b, t: (b, jnp.minimum(t, ctx_tile - 1), 0)),
            pl.BlockSpec((None, TM, D_MODEL), lambda b, t: (b, 0, 0)),
            tok(2048),
            _resident((D_MODEL, D_MODEL), lambda b, t: (0, 0)),
        ],
        out_specs=tok(D_MODEL),
        out_shape=jax.ShapeDtypeStruct((bsz, n, D_MODEL), F32),
        compiler_params=_cparams(("arbitrary", "arbitrary")),
        name="merge",
    )(xa, modsel, nw, gn_w, yf, yb, rg, at_lat, at_ctx, gates, w_out)


def _rope_tables(t, n):
    pos = jnp.arange(n, dtype=jnp.int32)
    is_lat = (pos < t)[:, None]
    lane = jnp.arange(128)
    inv_r = ROPE_BASE ** (-jnp.arange(64, dtype=F32) / 64)
    ang = pos.astype(F32)[:, None] * inv_r[None, :]
    cos, sin = jnp.cos(ang), jnp.sin(ang)
    cr = jnp.where(is_lat, jnp.concatenate([cos, cos], axis=1), 1.0)
    sr = jnp.where(is_lat, jnp.concatenate([-sin, sin], axis=1), 0.0)
    inv_d = ROPE_BASE ** (-jnp.arange(16, dtype=F32) / 16)
    row = (pos // GRID_W).astype(F32)[:, None] * inv_d[None, :]
    col = (pos % GRID_W).astype(F32)[:, None] * inv_d[None, :]
    cos64 = jnp.concatenate([jnp.cos(row)] * 2 + [jnp.cos(col)] * 2, axis=1)
    sin64 = jnp.concatenate([jnp.sin(row)] * 2 + [jnp.sin(col)] * 2, axis=1)
    cos128 = jnp.concatenate([cos64, cos64], axis=1)
    sin128 = jnp.concatenate([sin64, sin64], axis=1)
    upper = ((lane % 32) >= 16)[None, :]
    cd = jnp.where(is_lat, cos128, 1.0)
    sa = jnp.where(is_lat & upper, sin128, 0.0)
    sb = jnp.where(is_lat & ~upper, -sin128, 0.0)
    return cr, sr, cd, sa, sb


def kernel(x, c, ctx, c_ctx, w_ada, b_ada, norm_w, ffn_w_in, ffn_w_out, w_in, w_out,
           ret_decay_logit, ret_gn_w, diff_lambda, diff_subln_w):
    bsz, t, d = x.shape
    n_ctx = ctx.shape[1]
    depth = w_ada.shape[0]
    assert d == D_MODEL and n_ctx == TM and t % ATT_TQ == 0 and bsz + 1 <= 8
    n = t + n_ctx
    assert n % ATT_TK == 0
    n_tiles = n // TM
    ctx_tile = n_tiles - 1

    xa = jnp.concatenate([x, ctx], axis=1)
    cs = jnp.zeros((8, d), F32).at[:bsz].set(c).at[bsz].set(c_ctx)
    mod = _adaln(cs, w_ada, b_ada).reshape(depth, 8, N_MOD, d)
    modsel = jnp.stack([mod[:, :bsz], jnp.broadcast_to(mod[:, bsz:bsz + 1], (depth, bsz, N_MOD, d))], axis=2)
    tables = _rope_tables(t, n)

    cast = lambda w: w.astype(BF16)

    out = None
    for l in range(depth):
        last = l == depth - 1
        lam_init = 0.8 - 0.6 * math.exp(-0.3 * l)
        nw = norm_w[l]
        xa = _ffn(xa, modsel[l], nw, cast(ffn_w_in[l, 0]), cast(ffn_w_out[l, 0]),
                  mod_off=0, nw_off=0, n_tiles=n_tiles, ctx_tile=ctx_tile)
        qr, kr, vr, rg, qd, kd, vd, gates = _inproj(xa, modsel[l], nw, cast(w_in[l]), tables, ctx_tile=ctx_tile)
        yf, yb = _retention(ret_decay_logit[l], qr, kr, vr)
        at_lat = _attention(diff_lambda[l], diff_subln_w[l], qd, kd, vd, tq=ATT_TQ, tk=ATT_TK,
                            q_block0=0, n_q=t // ATT_TQ, kv_block0=0, n_kv=n, lam_init=lam_init)
        at_ctx = _attention(diff_lambda[l], diff_subln_w[l], qd, kd, vd, tq=TM, tk=TM,
                            q_block0=ctx_tile, n_q=1, kv_block0=ctx_tile, n_kv=TM, lam_init=lam_init)
        xa = _merge(xa, modsel[l], nw, ret_gn_w[l].reshape(1, d), yf, yb, rg, at_lat, at_ctx, gates,
                    cast(w_out[l]), ctx_tile=ctx_tile)
        if last:
            out = _ffn(xa, modsel[l], nw, cast(ffn_w_in[l, 1]), cast(ffn_w_out[l, 1]),
                       mod_off=6, nw_off=4, n_tiles=t // TM, ctx_tile=ctx_tile)
        else:
            xa = _ffn(xa, modsel[l], nw, cast(ffn_w_in[l, 1]), cast(ffn_w_out[l, 1]),
                      mod_off=6, nw_off=4, n_tiles=n_tiles, ctx_tile=ctx_tile)
    return out
```

```python
import functools
import math

import jax
import jax.numpy as jnp
from jax import lax
from jax.experimental import pallas as pl
from jax.experimental.pallas import tpu as pltpu

D_MODEL = 1024
D_FF = 2816
H_RET = 4
RET_DK = 128
RET_DV = 256
H_DIFF = 8
DIFF_DV = 128
DIFF_DH = 64
GRID_W = 64
ROPE_BASE = 10000.0
EPS = 1e-6
LOG2E = 1.4426950408889634
N_MOD = 9
IN_WIDTH = 8192
OFF_RQ, OFF_RK, OFF_RV, OFF_RG, OFF_DQ, OFF_DK, OFF_DV, OFF_GT = 0, 512, 1024, 2048, 3072, 4096, 5120, 6144

TM = 256
RET_BLOCK = 256
ATT_TQ = 512
ATT_TK = 1280
ATT_UNROLL = 4
ATT_RB = 256
ATT_VROWS = 144
VMEM_LIMIT = 56 * 1024 * 1024

BF16 = jnp.bfloat16
F32 = jnp.float32


def _cparams(sem):
    return pltpu.CompilerParams(dimension_semantics=sem, vmem_limit_bytes=VMEM_LIMIT)


def _resident(shape, index_map):
    return pl.BlockSpec(shape, index_map, pipeline_mode=pl.Buffered(1))


def _rms(x, g):
    return x * lax.rsqrt(jnp.mean(x * x, axis=-1, keepdims=True) + EPS) * g


def _mm(a, b):
    return jnp.dot(a, b, preferred_element_type=F32)


def _mm_nt(a, b):
    return lax.dot_general(a, b, (((1,), (1,)), ((), ())), preferred_element_type=F32)


def _mm_tn(a, b):
    return lax.dot_general(a, b, (((0,), (0,)), ((), ())), preferred_element_type=F32)


def _adaln_kernel(c_ref, w_ref, b_ref, o_ref):
    c = c_ref[...]
    s = c * jax.nn.sigmoid(c)
    o_ref[...] = _mm(s, w_ref[...]) + b_ref[...]


def _adaln(cs, w_ada, b_ada):
    depth = w_ada.shape[0]
    width = w_ada.shape[2]
    tn = 1024
    return pl.pallas_call(
        _adaln_kernel,
        grid=(depth, width // tn),
        in_specs=[
            pl.BlockSpec((8, D_MODEL), lambda l, j: (0, 0)),
            pl.BlockSpec((None, D_MODEL, tn), lambda l, j: (l, 0, j)),
            pl.BlockSpec((None, 1, tn), lambda l, j: (l, 0, j)),
        ],
        out_specs=pl.BlockSpec((None, 8, tn), lambda l, j: (l, 0, j)),
        out_shape=jax.ShapeDtypeStruct((depth, 8, width), F32),
        compiler_params=_cparams(("arbitrary", "arbitrary")),
        name="adaln",
    )(cs, w_ada, b_ada.reshape(depth, 1, width))


def _ffn_kernel(x_ref, mod_ref, nw_ref, win_ref, wout_ref, o_ref, *, mod_off, nw_off):
    x = x_ref[...]
    shift = mod_ref[mod_off:mod_off + 1, :]
    scale = mod_ref[mod_off + 1:mod_off + 2, :]
    gate = mod_ref[mod_off + 2:mod_off + 3, :]
    h = _rms(x, nw_ref[nw_off:nw_off + 1, :]) * (1.0 + scale) + shift
    hb = h.astype(BF16)
    a = _mm(hb, win_ref[:, :D_FF])
    b = _mm(hb, win_ref[:, D_FF:])
    u = (a * jax.nn.sigmoid(a) * b).astype(BF16)
    y = _mm(u, wout_ref[...])
    o_ref[...] = x + 0.5 * gate * _rms(y, nw_ref[nw_off + 1:nw_off + 2, :])


def _ffn(xa, modsel, nw, w_in, w_out, *, mod_off, nw_off, n_tiles, ctx_tile):
    bsz = xa.shape[0]
    kern = functools.partial(_ffn_kernel, mod_off=mod_off, nw_off=nw_off)
    return pl.pallas_call(
        kern,
        grid=(bsz, n_tiles),
        in_specs=[
            pl.BlockSpec((None, TM, D_MODEL), lambda b, t: (b, t, 0)),
            pl.BlockSpec((None, None, N_MOD, D_MODEL), lambda b, t: (b, t // ctx_tile, 0, 0)),
            _resident((6, D_MODEL), lambda b, t: (0, 0)),
            _resident((D_MODEL, 2 * D_FF), lambda b, t: (0, 0)),
            _resident((D_FF, D_MODEL), lambda b, t: (0, 0)),
        ],
        out_specs=pl.BlockSpec((None, TM, D_MODEL), lambda b, t: (b, t, 0)),
        out_shape=jax.ShapeDtypeStruct((bsz, n_tiles * TM, D_MODEL), F32),
        compiler_params=_cparams(("arbitrary", "arbitrary")),
        name="ffn",
    )(xa, modsel, nw, w_in, w_out)


def _inproj_kernel(x_ref, mod_ref, nw_ref, w_ref, cr_ref, sr_ref, cd_ref, sa_ref, sb_ref,
                   qr_ref, kr_ref, vr_ref, rg_ref, qd_ref, kd_ref, vd_ref, gt_ref):
    x = x_ref[...]
    h = _rms(x, nw_ref[2:3, :]) * (1.0 + mod_ref[4:5, :]) + mod_ref[3:4, :]
    hb = h.astype(BF16)
    cr, sr = cr_ref[...], sr_ref[...]
    cd, sa, sb = cd_ref[...], sa_ref[...], sb_ref[...]

    rq = _mm(hb, w_ref[:, OFF_RQ:OFF_RK])
    rk = _mm(hb, w_ref[:, OFF_RK:OFF_RV])
    for hd in range(H_RET):
        sl = slice(hd * RET_DK, (hd + 1) * RET_DK)
        q = rq[:, sl]
        k = rk[:, sl]
        qr_ref[:, sl] = (q * cr + pltpu.roll(q, 64, 1) * sr).astype(BF16)
        kr_ref[:, sl] = ((k * cr + pltpu.roll(k, 64, 1) * sr) * (RET_DK ** -0.5)).astype(BF16)
    vr_ref[...] = _mm(hb, w_ref[:, OFF_RV:OFF_RG]).astype(BF16)
    rg = _mm(hb, w_ref[:, OFF_RG:OFF_DQ])
    rg_ref[...] = (rg * jax.nn.sigmoid(rg)).astype(BF16)

    dq = _mm(hb, w_ref[:, OFF_DQ:OFF_DK])
    dk = _mm(hb, w_ref[:, OFF_DK:OFF_DV])
    dv = _mm(hb, w_ref[:, OFF_DV:OFF_GT])
    for hd in range(H_DIFF):
        sl = slice(hd * DIFF_DV, (hd + 1) * DIFF_DV)
        q = dq[:, sl]
        k = dk[:, sl]
        qrot = q * cd + pltpu.roll(q, 16, 1) * sa + pltpu.roll(q, 112, 1) * sb
        krot = k * cd + pltpu.roll(k, 16, 1) * sa + pltpu.roll(k, 112, 1) * sb
        qd_ref[hd] = (qrot * (DIFF_DH ** -0.5 * LOG2E)).astype(BF16)
        kd_ref[hd] = krot.astype(BF16)
        vd_ref[hd, 0:DIFF_DV, :] = dv[:, sl].T.astype(BF16)
        vd_ref[hd, DIFF_DV:ATT_VROWS, :] = jnp.ones((ATT_VROWS - DIFF_DV, TM), BF16)
    gt_ref[...] = jax.nn.sigmoid(_mm(hb, w_ref[:, OFF_GT:IN_WIDTH])).astype(BF16)


def _inproj(xa, modsel, nw, w_in, tables, *, ctx_tile):
    bsz, n, _ = xa.shape
    n_tiles = n // TM
    tok = lambda width: pl.BlockSpec((None, TM, width), lambda b, t: (b, t, 0))
    tab = pl.BlockSpec((TM, 128), lambda b, t: (t, 0))
    heads = pl.BlockSpec((None, H_DIFF, TM, DIFF_DV), lambda b, t: (b, 0, t, 0))
    tiles_per_chunk = ATT_TK // TM
    heads_t = pl.BlockSpec((None, H_DIFF, None, ATT_VROWS, TM),
                           lambda b, t: (b, 0, t // tiles_per_chunk, 0, t % tiles_per_chunk))
    sds = jax.ShapeDtypeStruct
    return pl.pallas_call(
        _inproj_kernel,
        grid=(bsz, n_tiles),
        in_specs=[
            tok(D_MODEL),
            pl.BlockSpec((None, None, N_MOD, D_MODEL), lambda b, t: (b, t // ctx_tile, 0, 0)),
            _resident((6, D_MODEL), lambda b, t: (0, 0)),
            _resident((D_MODEL, IN_WIDTH), lambda b, t: (0, 0)),
            tab, tab, tab, tab, tab,
        ],
        out_specs=[tok(512), tok(512), tok(1024), tok(1024), heads, heads, heads_t, tok(2048)],
        out_shape=[
            sds((bsz, n, 512), BF16), sds((bsz, n, 512), BF16),
            sds((bsz, n, 1024), BF16), sds((bsz, n, 1024), BF16),
            sds((bsz, H_DIFF, n, DIFF_DV), BF16), sds((bsz, H_DIFF, n, DIFF_DV), BF16),
            sds((bsz, H_DIFF, n // ATT_TK, ATT_VROWS, ATT_TK), BF16), sds((bsz, n, 2048), BF16),
        ],
        compiler_params=_cparams(("arbitrary", "arbitrary")),
        name="inproj",
    )(xa, modsel, nw, w_in, *tables)


def _retention_kernel(dl_ref, qf_ref, kf_ref, vf_ref, qb_ref, kb_ref, vb_ref, yf_ref, yb_ref,
                      st_ref, dm_ref, zt_ref, xi_ref):
    c = RET_BLOCK
    dl = dl_ref[...]
    lg = jnp.minimum(dl, 0.0) - jnp.log(1.0 + jnp.exp(-jnp.abs(dl)))

    @pl.when(pl.program_id(0) == 0)
    def _init():
        st_ref[...] = jnp.zeros_like(st_ref)
        ri = lax.broadcasted_iota(jnp.int32, (c, c), 0).astype(F32)
        ci = lax.broadcasted_iota(jnp.int32, (c, c), 1).astype(F32)
        pos = lax.broadcasted_iota(jnp.int32, (c, RET_DK), 0).astype(F32)
        for d in range(2):
            diff = ri - ci if d == 0 else ci - ri
            for hd in range(H_RET):
                g = lg[d:d + 1, hd:hd + 1]
                dm_ref[d, hd] = jnp.where(diff >= 0.0, jnp.exp(jnp.maximum(diff, 0.0) * g), 0.0)
                if d == 0:
                    zt_ref[d, hd] = jnp.exp((c - 1.0 - pos) * g)
                    xi_ref[d, hd] = jnp.exp((pos + 1.0) * g)
                else:
                    zt_ref[d, hd] = jnp.exp(pos * g)
                    xi_ref[d, hd] = jnp.exp((c - pos) * g)

    for d, (q_ref, k_ref, v_ref, y_ref) in enumerate(
            ((qf_ref, kf_ref, vf_ref, yf_ref), (qb_ref, kb_ref, vb_ref, yb_ref))):
        for hd in range(H_RET):
            g_blk = jnp.exp(float(c) * lg[d:d + 1, hd:hd + 1])
            for b in range(q_ref.shape[0]):
                q = q_ref[b, :, hd * RET_DK:(hd + 1) * RET_DK]
                k = k_ref[b, :, hd * RET_DK:(hd + 1) * RET_DK]
                v = v_ref[b, :, hd * RET_DV:(hd + 1) * RET_DV]
                s = _mm_nt(q, k) * dm_ref[d, hd]
                inner = _mm(s.astype(BF16), v)
                state = st_ref[b, d, hd]
                qx = (q.astype(F32) * xi_ref[d, hd]).astype(BF16)
                cross = _mm(qx, state.astype(BF16))
                y_ref[b, :, hd * RET_DV:(hd + 1) * RET_DV] = (inner + cross).astype(BF16)
                kz = (k.astype(F32) * zt_ref[d, hd]).astype(BF16)
                st_ref[b, d, hd] = g_blk * state + _mm_tn(kz, v)


def _retention(decay_logit, q, k, v):
    bsz, n, _ = q.shape
    nb = n // RET_BLOCK
    last = nb - 1
    fwd = lambda i: (0, jnp.where(i == 0, last, i - 1), 0)
    bwd = lambda i: (0, jnp.where(i == 0, last, last - i), 0)
    blk = lambda width, im: pl.BlockSpec((bsz, RET_BLOCK, width), im)
    y = jax.ShapeDtypeStruct((bsz, n, H_RET * RET_DV), BF16)
    return pl.pallas_call(
        _retention_kernel,
        grid=(nb,),
        in_specs=[
            pl.BlockSpec((2, H_RET), lambda i: (0, 0)),
            blk(512, fwd), blk(512, fwd), blk(1024, fwd),
            blk(512, bwd), blk(512, bwd), blk(1024, bwd),
        ],
        out_specs=[blk(1024, fwd), blk(1024, bwd)],
        out_shape=[y, y],
        scratch_shapes=[
            pltpu.VMEM((bsz, 2, H_RET, RET_DK, RET_DV), F32),
            pltpu.VMEM((2, H_RET, RET_BLOCK, RET_BLOCK), F32),
            pltpu.VMEM((2, H_RET, RET_BLOCK, RET_DK), F32),
            pltpu.VMEM((2, H_RET, RET_BLOCK, RET_DK), F32),
        ],
        compiler_params=_cparams(("arbitrary",)),
        name="retention",
    )(decay_logit, q, k, v, q, k, v)


def _attn_kernel(lam_ref, sw_ref, q_ref, k_ref, vt_ref, o_ref, qq_ref, m_ref, acc_ref, s_ref, cm_ref,
                 *, tq, tk, n_chunks, lam_init):
    qt = q_ref[...].astype(F32).T
    row = lax.broadcasted_iota(jnp.int32, qt.shape, 0)
    qq_ref[:, :tq] = jnp.where(row < DIFF_DH, qt, 0.0).astype(BF16)
    qq_ref[:, tq:] = jnp.where(row >= DIFF_DH, qt, 0.0).astype(BF16)
    m_ref[...] = jnp.full_like(m_ref, -jnp.inf)
    acc_ref[...] = jnp.zeros_like(acc_ref)

    def scores(ci, slot):
        start = pl.multiple_of(ci * tk, tk)
        s = _mm(k_ref[pl.ds(start, tk), :], qq_ref[...])
        s_ref[slot] = s
        cm_ref[slot] = jnp.max(s, axis=0, keepdims=True)

    def softmax_pv(ci, slot):
        s = s_ref[slot]
        m_old = m_ref[...]
        m_new = jnp.maximum(m_old, cm_ref[slot])
        alpha = jnp.exp2(m_old - m_new)
        p = jnp.exp2(s - m_new)
        acc_ref[...] = alpha * acc_ref[...] + _mm(vt_ref[ci], p.astype(BF16))
        m_ref[...] = m_new

    scores(0, 0)
    n_loop = (n_chunks - 1) // ATT_UNROLL

    n_rb = tk // ATT_RB

    def fused(c_next, c_cur, slot):
        nstart = pl.multiple_of(c_next * tk, tk)
        m_old = m_ref[...]
        m_new = jnp.maximum(m_old, cm_ref[slot])
        alpha = jnp.exp2(m_old - m_new)
        pv = None
        cmax = None
        for r in range(n_rb):
            rows = slice(r * ATT_RB, (r + 1) * ATT_RB)
            sn = _mm(k_ref[pl.ds(nstart + r * ATT_RB, ATT_RB), :], qq_ref[...])
            s_ref[1 - slot, rows, :] = sn
            cm_r = jnp.max(sn, axis=0, keepdims=True)
            cmax = cm_r if cmax is None else jnp.maximum(cmax, cm_r)
            p = jnp.exp2(s_ref[slot, rows, :] - m_new)
            pv_r = _mm(vt_ref[c_cur, :, rows], p.astype(BF16))
            pv = pv_r if pv is None else pv + pv_r
        cm_ref[1 - slot] = cmax
        acc_ref[...] = alpha * acc_ref[...] + pv
        m_ref[...] = m_new

    def body(j, carry):
        c0 = ATT_UNROLL * j
        for u in range(ATT_UNROLL):
            fused(c0 + u + 1, c0 + u, u % 2)
        return carry

    if n_loop > 0:
        lax.fori_loop(0, n_loop, body, 0)
    for c in range(n_loop * ATT_UNROLL, n_chunks):
        if c + 1 < n_chunks:
            scores(c + 1, (c + 1) % 2)
        softmax_pv(c, c % 2)

    lv = lam_ref[...]
    lam = (jnp.exp(jnp.sum(lv[0:1, :] * lv[1:2, :], axis=1, keepdims=True))
           - jnp.exp(jnp.sum(lv[2:3, :] * lv[3:4, :], axis=1, keepdims=True)) + lam_init)
    acc = acc_ref[0:DIFF_DV, :]
    l = acc_ref[DIFF_DV:DIFF_DV + 1, :]
    o = acc[:, :tq] / l[:, :tq] - lam * (acc[:, tq:] / l[:, tq:])
    o = o * lax.rsqrt(jnp.mean(o * o, axis=0, keepdims=True) + EPS) * sw_ref[...] * (1.0 - lam_init)
    o_ref[...] = o.T.astype(BF16)


def _attention(lam_vec, subln_w, q, k, vt, *, tq, tk, q_block0, n_q, kv_block0, n_kv, lam_init):
    bsz = q.shape[0]
    n_chunks = n_kv // tk
    if tk == ATT_TK:
        assert kv_block0 == 0 and n_chunks == vt.shape[2]
        vt_spec = pl.BlockSpec((None, None, n_chunks, ATT_VROWS, tk), lambda b, h, i: (b, h, 0, 0, 0))
    else:
        assert n_chunks == 1 and (kv_block0 + 1) * tk == vt.shape[2] * ATT_TK
        vt_spec = pl.BlockSpec((None, None, 1, ATT_VROWS, tk),
                               lambda b, h, i: (b, h, vt.shape[2] - 1, 0, ATT_TK // tk - 1))
    kern = functools.partial(_attn_kernel, tq=tq, tk=tk, n_chunks=n_chunks, lam_init=lam_init)
    return pl.pallas_call(
        kern,
        grid=(bsz, H_DIFF, n_q),
        in_specs=[
            pl.BlockSpec((4, DIFF_DH), lambda b, h, i: (0, 0)),
            pl.BlockSpec((DIFF_DV, 1), lambda b, h, i: (0, 0)),
            pl.BlockSpec((None, None, tq, DIFF_DV), lambda b, h, i: (b, h, q_block0 + i, 0)),
            pl.BlockSpec((None, None, n_kv, DIFF_DV), lambda b, h, i: (b, h, kv_block0, 0)),
            vt_spec,
        ],
        out_specs=pl.BlockSpec((None, tq, DIFF_DV), lambda b, h, i: (b, i, h)),
        out_shape=jax.ShapeDtypeStruct((bsz, n_q * tq, H_DIFF * DIFF_DV), BF16),
        scratch_shapes=[
            pltpu.VMEM((DIFF_DV, 2 * tq), BF16),
            pltpu.VMEM((1, 2 * tq), F32),
            pltpu.VMEM((ATT_VROWS, 2 * tq), F32),
            pltpu.VMEM((2, tk, 2 * tq), F32),
            pltpu.VMEM((2, 1, 2 * tq), F32),
        ],
        compiler_params=_cparams(("arbitrary", "arbitrary", "arbitrary")),
        name="diffattn",
    )(lam_vec, subln_w.reshape(DIFF_DV, 1), q, k, vt)


def _merge_kernel(x_ref, mod_ref, nw_ref, gn_ref, yf_ref, yb_ref, rg_ref, al_ref, ac_ref, gt_ref, w_ref, o_ref,
                  *, ctx_tile):
    x = x_ref[...]
    is_ctx = pl.program_id(1) == ctx_tile
    diff = jnp.where(is_ctx, ac_ref[...], al_ref[...]).astype(F32)
    y = yf_ref[...].astype(F32) + yb_ref[...].astype(F32)
    parts = []
    for hd in range(H_RET):
        sl = slice(hd * RET_DV, (hd + 1) * RET_DV)
        yh = y[:, sl]
        mu = jnp.mean(yh, axis=-1, keepdims=True)
        yc = yh - mu
        var = jnp.mean(yc * yc, axis=-1, keepdims=True)
        parts.append(yc * lax.rsqrt(var + EPS) * gn_ref[:, sl])
    ret = jnp.concatenate(parts, axis=1) * rg_ref[...].astype(F32)
    ga = gt_ref[:, :D_MODEL].astype(F32)
    gb = gt_ref[:, D_MODEL:].astype(F32)
    z = (ga * ret + gb * diff).astype(BF16)
    yo = _mm(z, w_ref[...])
    o_ref[...] = x + mod_ref[5:6, :] * _rms(yo, nw_ref[3:4, :])


def _merge(xa, modsel, nw, gn_w, yf, yb, rg, at_lat, at_ctx, gates, w_out, *, ctx_tile):
    bsz, n, _ = xa.shape
    n_tiles = n // TM
    tok = lambda width: pl.BlockSpec((None, TM, width), lambda b, t: (b, t, 0))
    kern = functools.partial(_merge_kernel, ctx_tile=ctx_tile)
    return pl.pallas_call(
        kern,
        grid=(bsz, n_tiles),
        in_specs=[
            tok(D_MODEL),
            pl.BlockSpec((None, None, N_MOD, D_MODEL), lambda b, t: (b, t // ctx_tile, 0, 0)),
            _resident((6, D_MODEL), lambda b, t: (0, 0)),
            _resident((1, D_MODEL), lambda b, t: (0, 0)),
            tok(1024), tok(1024), tok(1024),
            pl.BlockSpec((None, TM, D_MODEL), lambda b, t: (b, jnp.minimum(t, ctx_tile - 1), 0)),
            pl.BlockSpec((None, TM, D_MODEL), lambda b, t: (b, 0, 0)),
            tok(2048),
            _resident((D_MODEL, D_MODEL), lambda b, t: (0, 0)),
        ],
        out_specs=tok(D_MODEL),
        out_shape=jax.ShapeDtypeStruct((bsz, n, D_MODEL), F32),
        compiler_params=_cparams(("arbitrary", "arbitrary")),
        name="merge",
    )(xa, modsel, nw, gn_w, yf, yb, rg, at_lat, at_ctx, gates, w_out)


def _merge_ffn_kernel(x_ref, modm_ref, modf_ref, nw_ref, gn_ref, yf_ref, yb_ref, rg_ref, al_ref, ac_ref, gt_ref,
                      wo_ref, win_ref, wout_ref, o_ref, xm_ref, *, ctx_tile, n_tiles):
    t = pl.program_id(1)

    @pl.when(t == 0)
    def _init():
        xm_ref[...] = jnp.zeros_like(xm_ref)

    xp = xm_ref[...]
    h = _rms(xp, nw_ref[4:5, :]) * (1.0 + modf_ref[7:8, :]) + modf_ref[6:7, :]
    hb = h.astype(BF16)
    a = _mm(hb, win_ref[:, :D_FF])
    b = _mm(hb, win_ref[:, D_FF:])
    u = (a * jax.nn.sigmoid(a) * b).astype(BF16)
    yf2 = _mm(u, wout_ref[...])
    o_ref[...] = xp + 0.5 * modf_ref[8:9, :] * _rms(yf2, nw_ref[5:6, :])

    x = x_ref[...]
    is_ctx = jnp.logical_and(t == ctx_tile, n_tiles > ctx_tile)
    diff = jnp.where(is_ctx, ac_ref[...], al_ref[...]).astype(F32)
    y = yf_ref[...].astype(F32) + yb_ref[...].astype(F32)
    parts = []
    for hd in range(H_RET):
        sl = slice(hd * RET_DV, (hd + 1) * RET_DV)
        yh = y[:, sl]
        mu = jnp.mean(yh, axis=-1, keepdims=True)
        yc = yh - mu
        var = jnp.mean(yc * yc, axis=-1, keepdims=True)
        parts.append(yc * lax.rsqrt(var + EPS) * gn_ref[:, sl])
    ret = jnp.concatenate(parts, axis=1) * rg_ref[...].astype(F32)
    ga = gt_ref[:, :D_MODEL].astype(F32)
    gb = gt_ref[:, D_MODEL:].astype(F32)
    z = (ga * ret + gb * diff).astype(BF16)
    yo = _mm(z, wo_ref[...])
    xm_ref[...] = x + modm_ref[5:6, :] * _rms(yo, nw_ref[3:4, :])


def _merge_ffn(xa, modsel, nw, gn_w, yf, yb, rg, at_lat, at_ctx, gates, w_out, ffn_w_in, ffn_w_out, *, ctx_tile, n_tiles):
    bsz = xa.shape[0]
    cur = lambda t: jnp.minimum(t, n_tiles - 1)
    prev = lambda t: jnp.maximum(t - 1, 0)
    tok = lambda width: pl.BlockSpec((None, TM, width), lambda b, t: (b, cur(t), 0))
    mod_spec = lambda sel: pl.BlockSpec((None, None, N_MOD, D_MODEL), lambda b, t: (b, sel(t) // ctx_tile, 0, 0))
    kern = functools.partial(_merge_ffn_kernel, ctx_tile=ctx_tile, n_tiles=n_tiles)
    return pl.pallas_call(
        kern,
        grid=(bsz, n_tiles + 1),
        in_specs=[
            tok(D_MODEL), mod_spec(cur), mod_spec(prev),
            _resident((6, D_MODEL), lambda b, t: (0, 0)),
            _resident((1, D_MODEL), lambda b, t: (0, 0)),
            tok(1024), tok(1024), tok(1024),
            pl.BlockSpec((None, TM, D_MODEL), lambda b, t: (b, jnp.minimum(t, ctx_tile - 1), 0)),
            pl.BlockSpec((None, TM, D_MODEL), lambda b, t: (b, 0, 0)),
            tok(2048),
            _resident((D_MODEL, D_MODEL), lambda b, t: (0, 0)),
            _resident((D_MODEL, 2 * D_FF), lambda b, t: (0, 0)),
            _resident((D_FF, D_MODEL), lambda b, t: (0, 0)),
        ],
        out_specs=pl.BlockSpec((None, TM, D_MODEL), lambda b, t: (b, prev(t), 0)),
        out_shape=jax.ShapeDtypeStruct((bsz, n_tiles * TM, D_MODEL), F32),
        scratch_shapes=[pltpu.VMEM((TM, D_MODEL), F32)],
        compiler_params=_cparams(("arbitrary", "arbitrary")),
        name="merge_ffn",
    )(xa, modsel, modsel, nw, gn_w, yf, yb, rg, at_lat, at_ctx, gates, w_out, ffn_w_in, ffn_w_out)


def _rope_tables(t, n):
    pos = jnp.arange(n, dtype=jnp.int32)
    is_lat = (pos < t)[:, None]
    lane = jnp.arange(128)
    inv_r = ROPE_BASE ** (-jnp.arange(64, dtype=F32) / 64)
    ang = pos.astype(F32)[:, None] * inv_r[None, :]
    cos, sin = jnp.cos(ang), jnp.sin(ang)
    cr = jnp.where(is_lat, jnp.concatenate([cos, cos], axis=1), 1.0)
    sr = jnp.where(is_lat, jnp.concatenate([-sin, sin], axis=1), 0.0)
    inv_d = ROPE_BASE ** (-jnp.arange(16, dtype=F32) / 16)
    row = (pos // GRID_W).astype(F32)[:, None] * inv_d[None, :]
    col = (pos % GRID_W).astype(F32)[:, None] * inv_d[None, :]
    cos64 = jnp.concatenate([jnp.cos(row)] * 2 + [jnp.cos(col)] * 2, axis=1)
    sin64 = jnp.concatenate([jnp.sin(row)] * 2 + [jnp.sin(col)] * 2, axis=1)
    cos128 = jnp.concatenate([cos64, cos64], axis=1)
    sin128 = jnp.concatenate([sin64, sin64], axis=1)
    upper = ((lane % 32) >= 16)[None, :]
    cd = jnp.where(is_lat, cos128, 1.0)
    sa = jnp.where(is_lat & upper, sin128, 0.0)
    sb = jnp.where(is_lat & ~upper, -sin128, 0.0)
    return cr, sr, cd, sa, sb


def kernel(x, c, ctx, c_ctx, w_ada, b_ada, norm_w, ffn_w_in, ffn_w_out, w_in, w_out,
           ret_decay_logit, ret_gn_w, diff_lambda, diff_subln_w):
    bsz, t, d = x.shape
    n_ctx = ctx.shape[1]
    depth = w_ada.shape[0]
    assert d == D_MODEL and n_ctx == TM and t % ATT_TQ == 0 and bsz + 1 <= 8
    n = t + n_ctx
    assert n % ATT_TK == 0
    n_tiles = n // TM
    ctx_tile = n_tiles - 1

    xa = jnp.concatenate([x, ctx], axis=1)
    cs = jnp.zeros((8, d), F32).at[:bsz].set(c).at[bsz].set(c_ctx)
    mod = _adaln(cs, w_ada, b_ada).reshape(depth, 8, N_MOD, d)
    modsel = jnp.stack([mod[:, :bsz], jnp.broadcast_to(mod[:, bsz:bsz + 1], (depth, bsz, N_MOD, d))], axis=2)
    tables = _rope_tables(t, n)

    cast = lambda w: w.astype(BF16)

    out = None
    for l in range(depth):
        last = l == depth - 1
        lam_init = 0.8 - 0.6 * math.exp(-0.3 * l)
        nw = norm_w[l]
        xa = _ffn(xa, modsel[l], nw, cast(ffn_w_in[l, 0]), cast(ffn_w_out[l, 0]),
                  mod_off=0, nw_off=0, n_tiles=n_tiles, ctx_tile=ctx_tile)
        qr, kr, vr, rg, qd, kd, vd, gates = _inproj(xa, modsel[l], nw, cast(w_in[l]), tables, ctx_tile=ctx_tile)
        yf, yb = _retention(ret_decay_logit[l], qr, kr, vr)
        at_lat = _attention(diff_lambda[l], diff_subln_w[l], qd, kd, vd, tq=ATT_TQ, tk=ATT_TK,
                            q_block0=0, n_q=t // ATT_TQ, kv_block0=0, n_kv=n, lam_init=lam_init)
        at_ctx = _attention(diff_lambda[l], diff_subln_w[l], qd, kd, vd, tq=TM, tk=TM,
                            q_block0=ctx_tile, n_q=1, kv_block0=ctx_tile, n_kv=TM, lam_init=lam_init)
        res = _merge_ffn(xa, modsel[l], nw, ret_gn_w[l].reshape(1, d), yf, yb, rg, at_lat, at_ctx, gates,
                         cast(w_out[l]), cast(ffn_w_in[l, 1]), cast(ffn_w_out[l, 1]),
                         ctx_tile=ctx_tile, n_tiles=(t // TM) if last else n_tiles)
        if last:
            out = res
        else:
            xa = res
    return out
```

```python
import functools
import math

import jax
import jax.numpy as jnp
from jax import lax
from jax.experimental import pallas as pl
from jax.experimental.pallas import tpu as pltpu

D_MODEL = 1024
D_FF = 2816
H_RET = 4
RET_DK = 128
RET_DV = 256
H_DIFF = 8
DIFF_DV = 128
DIFF_DH = 64
GRID_W = 64
ROPE_BASE = 10000.0
EPS = 1e-6
LOG2E = 1.4426950408889634
N_MOD = 9
IN_WIDTH = 8192
OFF_RQ, OFF_RK, OFF_RV, OFF_RG, OFF_DQ, OFF_DK, OFF_DV, OFF_GT = 0, 512, 1024, 2048, 3072, 4096, 5120, 6144

TM = 256
RET_BLOCK = 256
ATT_TQ = 512
ATT_TK = 1280
ATT_UNROLL = 4
ATT_RB = 256
ATT_VROWS = 144
VMEM_LIMIT = 56 * 1024 * 1024

BF16 = jnp.bfloat16
F32 = jnp.float32


def _cparams(sem):
    return pltpu.CompilerParams(dimension_semantics=sem, vmem_limit_bytes=VMEM_LIMIT)


def _resident(shape, index_map):
    return pl.BlockSpec(shape, index_map, pipeline_mode=pl.Buffered(1))


def _rms(x, g):
    return x * lax.rsqrt(jnp.mean(x * x, axis=-1, keepdims=True) + EPS) * g


def _mm(a, b):
    return jnp.dot(a, b, preferred_element_type=F32)


def _mm_nt(a, b):
    return lax.dot_general(a, b, (((1,), (1,)), ((), ())), preferred_element_type=F32)


def _mm_tn(a, b):
    return lax.dot_general(a, b, (((0,), (0,)), ((), ())), preferred_element_type=F32)


def _adaln_kernel(c_ref, w_ref, b_ref, o_ref):
    c = c_ref[...]
    s = c * jax.nn.sigmoid(c)
    o_ref[...] = _mm(s, w_ref[...]) + b_ref[...]


def _adaln(cs, w_ada, b_ada):
    depth = w_ada.shape[0]
    width = w_ada.shape[2]
    tn = 1024
    return pl.pallas_call(
        _adaln_kernel,
        grid=(depth, width // tn),
        in_specs=[
            pl.BlockSpec((8, D_MODEL), lambda l, j: (0, 0)),
            pl.BlockSpec((None, D_MODEL, tn), lambda l, j: (l, 0, j)),
            pl.BlockSpec((None, 1, tn), lambda l, j: (l, 0, j)),
        ],
        out_specs=pl.BlockSpec((None, 8, tn), lambda l, j: (l, 0, j)),
        out_shape=jax.ShapeDtypeStruct((depth, 8, width), F32),
        compiler_params=_cparams(("arbitrary", "arbitrary")),
        name="adaln",
    )(cs, w_ada, b_ada.reshape(depth, 1, width))


def _ffn_kernel(x_ref, mod_ref, nw_ref, win_ref, wout_ref, o_ref, *, mod_off, nw_off):
    x = x_ref[...]
    shift = mod_ref[mod_off:mod_off + 1, :]
    scale = mod_ref[mod_off + 1:mod_off + 2, :]
    gate = mod_ref[mod_off + 2:mod_off + 3, :]
    h = _rms(x, nw_ref[nw_off:nw_off + 1, :]) * (1.0 + scale) + shift
    hb = h.astype(BF16)
    a = _mm(hb, win_ref[:, :D_FF])
    b = _mm(hb, win_ref[:, D_FF:])
    u = (a * jax.nn.sigmoid(a) * b).astype(BF16)
    y = _mm(u, wout_ref[...])
    o_ref[...] = x + 0.5 * gate * _rms(y, nw_ref[nw_off + 1:nw_off + 2, :])


def _ffn(xa, modsel, nw, w_in, w_out, *, mod_off, nw_off, n_tiles, ctx_tile):
    bsz = xa.shape[0]
    kern = functools.partial(_ffn_kernel, mod_off=mod_off, nw_off=nw_off)
    return pl.pallas_call(
        kern,
        grid=(bsz, n_tiles),
        in_specs=[
            pl.BlockSpec((None, TM, D_MODEL), lambda b, t: (b, t, 0)),
            pl.BlockSpec((None, None, N_MOD, D_MODEL), lambda b, t: (b, t // ctx_tile, 0, 0)),
            _resident((6, D_MODEL), lambda b, t: (0, 0)),
            _resident((D_MODEL, 2 * D_FF), lambda b, t: (0, 0)),
            _resident((D_FF, D_MODEL), lambda b, t: (0, 0)),
        ],
        out_specs=pl.BlockSpec((None, TM, D_MODEL), lambda b, t: (b, t, 0)),
        out_shape=jax.ShapeDtypeStruct((bsz, n_tiles * TM, D_MODEL), F32),
        compiler_params=_cparams(("arbitrary", "arbitrary")),
        name="ffn",
    )(xa, modsel, nw, w_in, w_out)


def _inproj_kernel(x_ref, mod_ref, nw_ref, w_ref, cr_ref, sr_ref, cd_ref, sa_ref, sb_ref,
                   qr_ref, kr_ref, vr_ref, rg_ref, qd_ref, kd_ref, vd_ref, gt_ref):
    x = x_ref[...]
    h = _rms(x, nw_ref[2:3, :]) * (1.0 + mod_ref[4:5, :]) + mod_ref[3:4, :]
    hb = h.astype(BF16)
    cr, sr = cr_ref[...], sr_ref[...]
    cd, sa, sb = cd_ref[...], sa_ref[...], sb_ref[...]

    rq = _mm(hb, w_ref[:, OFF_RQ:OFF_RK])
    rk = _mm(hb, w_ref[:, OFF_RK:OFF_RV])
    for hd in range(H_RET):
        sl = slice(hd * RET_DK, (hd + 1) * RET_DK)
        q = rq[:, sl]
        k = rk[:, sl]
        qr_ref[:, sl] = (q * cr + pltpu.roll(q, 64, 1) * sr).astype(BF16)
        kr_ref[:, sl] = ((k * cr + pltpu.roll(k, 64, 1) * sr) * (RET_DK ** -0.5)).astype(BF16)
    vr_ref[...] = _mm(hb, w_ref[:, OFF_RV:OFF_RG]).astype(BF16)
    rg = _mm(hb, w_ref[:, OFF_RG:OFF_DQ])
    rg_ref[...] = (rg * jax.nn.sigmoid(rg)).astype(BF16)

    dq = _mm(hb, w_ref[:, OFF_DQ:OFF_DK])
    dk = _mm(hb, w_ref[:, OFF_DK:OFF_DV])
    dv = _mm(hb, w_ref[:, OFF_DV:OFF_GT])
    for hd in range(H_DIFF):
        sl = slice(hd * DIFF_DV, (hd + 1) * DIFF_DV)
        q = dq[:, sl]
        k = dk[:, sl]
        qrot = q * cd + pltpu.roll(q, 16, 1) * sa + pltpu.roll(q, 112, 1) * sb
        krot = k * cd + pltpu.roll(k, 16, 1) * sa + pltpu.roll(k, 112, 1) * sb
        qd_ref[hd] = (qrot * (DIFF_DH ** -0.5 * LOG2E)).astype(BF16)
        kd_ref[hd] = krot.astype(BF16)
        vd_ref[hd, 0:DIFF_DV, :] = dv[:, sl].T.astype(BF16)
        vd_ref[hd, DIFF_DV:ATT_VROWS, :] = jnp.ones((ATT_VROWS - DIFF_DV, TM), BF16)
    gt_ref[...] = jax.nn.sigmoid(_mm(hb, w_ref[:, OFF_GT:IN_WIDTH])).astype(BF16)


def _inproj(xa, modsel, nw, w_in, tables, *, ctx_tile):
    bsz, n, _ = xa.shape
    n_tiles = n // TM
    tok = lambda width: pl.BlockSpec((None, TM, width), lambda b, t: (b, t, 0))
    tab = pl.BlockSpec((TM, 128), lambda b, t: (t, 0))
    heads = pl.BlockSpec((None, H_DIFF, TM, DIFF_DV), lambda b, t: (b, 0, t, 0))
    tiles_per_chunk = ATT_TK // TM
    heads_t = pl.BlockSpec((None, H_DIFF, None, ATT_VROWS, TM),
                           lambda b, t: (b, 0, t // tiles_per_chunk, 0, t % tiles_per_chunk))
    sds = jax.ShapeDtypeStruct
    return pl.pallas_call(
        _inproj_kernel,
        grid=(bsz, n_tiles),
        in_specs=[
            tok(D_MODEL),
            pl.BlockSpec((None, None, N_MOD, D_MODEL), lambda b, t: (b, t // ctx_tile, 0, 0)),
            _resident((6, D_MODEL), lambda b, t: (0, 0)),
            _resident((D_MODEL, IN_WIDTH), lambda b, t: (0, 0)),
            tab, tab, tab, tab, tab,
        ],
        out_specs=[tok(512), tok(512), tok(1024), tok(1024), heads, heads, heads_t, tok(2048)],
        out_shape=[
            sds((bsz, n, 512), BF16), sds((bsz, n, 512), BF16),
            sds((bsz, n, 1024), BF16), sds((bsz, n, 1024), BF16),
            sds((bsz, H_DIFF, n, DIFF_DV), BF16), sds((bsz, H_DIFF, n, DIFF_DV), BF16),
            sds((bsz, H_DIFF, n // ATT_TK, ATT_VROWS, ATT_TK), BF16), sds((bsz, n, 2048), BF16),
        ],
        compiler_params=_cparams(("arbitrary", "arbitrary")),
        name="inproj",
    )(xa, modsel, nw, w_in, *tables)


def _retention_kernel(dl_ref, qf_ref, kf_ref, vf_ref, qb_ref, kb_ref, vb_ref, yf_ref, yb_ref,
                      st_ref, dm_ref, zt_ref, xi_ref):
    c = RET_BLOCK
    dl = dl_ref[...]
    lg = jnp.minimum(dl, 0.0) - jnp.log(1.0 + jnp.exp(-jnp.abs(dl)))

    @pl.when(pl.program_id(0) == 0)
    def _init():
        st_ref[...] = jnp.zeros_like(st_ref)
        ri = lax.broadcasted_iota(jnp.int32, (c, c), 0).astype(F32)
        ci = lax.broadcasted_iota(jnp.int32, (c, c), 1).astype(F32)
        pos = lax.broadcasted_iota(jnp.int32, (c, RET_DK), 0).astype(F32)
        for d in range(2):
            diff = ri - ci if d == 0 else ci - ri
            for hd in range(H_RET):
                g = lg[d:d + 1, hd:hd + 1]
                dm_ref[d, hd] = jnp.where(diff >= 0.0, jnp.exp(jnp.maximum(diff, 0.0) * g), 0.0)
                if d == 0:
                    zt_ref[d, hd] = jnp.exp((c - 1.0 - pos) * g)
                    xi_ref[d, hd] = jnp.exp((pos + 1.0) * g)
                else:
                    zt_ref[d, hd] = jnp.exp(pos * g)
                    xi_ref[d, hd] = jnp.exp((c - pos) * g)

    for d, (q_ref, k_ref, v_ref, y_ref) in enumerate(
            ((qf_ref, kf_ref, vf_ref, yf_ref), (qb_ref, kb_ref, vb_ref, yb_ref))):
        for hd in range(H_RET):
            g_blk = jnp.exp(float(c) * lg[d:d + 1, hd:hd + 1])
            for b in range(q_ref.shape[0]):
                q = q_ref[b, :, hd * RET_DK:(hd + 1) * RET_DK]
                k = k_ref[b, :, hd * RET_DK:(hd + 1) * RET_DK]
                v = v_ref[b, :, hd * RET_DV:(hd + 1) * RET_DV]
                s = _mm_nt(q, k) * dm_ref[d, hd]
                inner = _mm(s.astype(BF16), v)
                state = st_ref[b, d, hd]
                qx = (q.astype(F32) * xi_ref[d, hd]).astype(BF16)
                cross = _mm(qx, state.astype(BF16))
                y_ref[b, :, hd * RET_DV:(hd + 1) * RET_DV] = (inner + cross).astype(BF16)
                kz = (k.astype(F32) * zt_ref[d, hd]).astype(BF16)
                st_ref[b, d, hd] = g_blk * state + _mm_tn(kz, v)


def _retention(decay_logit, q, k, v):
    bsz, n, _ = q.shape
    nb = n // RET_BLOCK
    last = nb - 1
    fwd = lambda i: (0, jnp.where(i == 0, last, i - 1), 0)
    bwd = lambda i: (0, jnp.where(i == 0, last, last - i), 0)
    blk = lambda width, im: pl.BlockSpec((bsz, RET_BLOCK, width), im)
    y = jax.ShapeDtypeStruct((bsz, n, H_RET * RET_DV), BF16)
    return pl.pallas_call(
        _retention_kernel,
        grid=(nb,),
        in_specs=[
            pl.BlockSpec((2, H_RET), lambda i: (0, 0)),
            blk(512, fwd), blk(512, fwd), blk(1024, fwd),
            blk(512, bwd), blk(512, bwd), blk(1024, bwd),
        ],
        out_specs=[blk(1024, fwd), blk(1024, bwd)],
        out_shape=[y, y],
        scratch_shapes=[
            pltpu.VMEM((bsz, 2, H_RET, RET_DK, RET_DV), F32),
            pltpu.VMEM((2, H_RET, RET_BLOCK, RET_BLOCK), F32),
            pltpu.VMEM((2, H_RET, RET_BLOCK, RET_DK), F32),
            pltpu.VMEM((2, H_RET, RET_BLOCK, RET_DK), F32),
        ],
        compiler_params=_cparams(("arbitrary",)),
        name="retention",
    )(decay_logit, q, k, v, q, k, v)


def _attn_kernel(lam_ref, sw_ref, q_ref, qn_ref, k_ref, vt_ref, o_ref, qq_ref, m_ref, acc_ref, s_ref, cm_ref,
                 *, tq, tk, n_chunks, n_q, lam_init):
    i_q = pl.program_id(2)
    n_rb = tk // ATT_RB

    def build_qq(src_ref, par):
        qt = src_ref[...].astype(F32).T
        row = lax.broadcasted_iota(jnp.int32, qt.shape, 0)
        qq_ref[par, :, :tq] = jnp.where(row < DIFF_DH, qt, 0.0).astype(BF16)
        qq_ref[par, :, tq:] = jnp.where(row >= DIFF_DH, qt, 0.0).astype(BF16)

    def scores(ci, slot, qpar):
        start = pl.multiple_of(ci * tk, tk)
        s = _mm(k_ref[pl.ds(start, tk), :], qq_ref[qpar])
        s_ref[slot] = s
        cm_ref[slot] = jnp.max(s, axis=0, keepdims=True)

    def softmax_pv(ci, slot):
        s = s_ref[slot]
        m_old = m_ref[...]
        m_new = jnp.maximum(m_old, cm_ref[slot])
        alpha = jnp.exp2(m_old - m_new)
        p = jnp.exp2(s - m_new)
        acc_ref[...] = alpha * acc_ref[...] + _mm(vt_ref[ci], p.astype(BF16))
        m_ref[...] = m_new

    def fused(c_next, c_cur, slot, qpar_next):
        nstart = pl.multiple_of(c_next * tk, tk)
        m_old = m_ref[...]
        m_new = jnp.maximum(m_old, cm_ref[slot])
        alpha = jnp.exp2(m_old - m_new)
        pv = None
        cmax = None
        for r in range(n_rb):
            rows = slice(r * ATT_RB, (r + 1) * ATT_RB)
            sn = _mm(k_ref[pl.ds(nstart + r * ATT_RB, ATT_RB), :], qq_ref[qpar_next])
            s_ref[1 - slot, rows, :] = sn
            cm_r = jnp.max(sn, axis=0, keepdims=True)
            cmax = cm_r if cmax is None else jnp.maximum(cmax, cm_r)
            p = jnp.exp2(s_ref[slot, rows, :] - m_new)
            pv_r = _mm(vt_ref[c_cur, :, rows], p.astype(BF16))
            pv = pv_r if pv is None else pv + pv_r
        cm_ref[1 - slot] = cmax
        acc_ref[...] = alpha * acc_ref[...] + pv
        m_ref[...] = m_new

    def tile(par):
        @pl.when(i_q == 0)
        def _first_tile():
            build_qq(q_ref, par)
            scores(0, par, par)

        m_ref[...] = jnp.full_like(m_ref, -jnp.inf)
        acc_ref[...] = jnp.zeros_like(acc_ref)

        n_steady = n_chunks - 1
        n_loop = n_steady // ATT_UNROLL

        def body(j, carry):
            c0 = ATT_UNROLL * j
            for u in range(ATT_UNROLL):
                fused(c0 + u + 1, c0 + u, (u + par) % 2, par)
            return carry

        if n_loop > 0:
            lax.fori_loop(0, n_loop, body, 0)
        for c in range(n_loop * ATT_UNROLL, n_steady):
            fused(c + 1, c, (c + par) % 2, par)

        last = n_chunks - 1
        last_slot = (last + par) % 2

        @pl.when(i_q + 1 < n_q)
        def _chain():
            build_qq(qn_ref, 1 - par)
            if last_slot == par:
                fused(0, last, last_slot, 1 - par)
            else:
                softmax_pv(last, last_slot)
                scores(0, 1 - par, 1 - par)

        @pl.when(i_q + 1 >= n_q)
        def _end():
            softmax_pv(last, last_slot)

    if n_q == 1:
        tile(0)
    else:
        @pl.when(i_q % 2 == 0)
        def _even():
            tile(0)

        @pl.when(i_q % 2 == 1)
        def _odd():
            tile(1)

    lv = lam_ref[...]
    lam = (jnp.exp(jnp.sum(lv[0:1, :] * lv[1:2, :], axis=1, keepdims=True))
           - jnp.exp(jnp.sum(lv[2:3, :] * lv[3:4, :], axis=1, keepdims=True)) + lam_init)
    acc = acc_ref[0:DIFF_DV, :]
    l = acc_ref[DIFF_DV:DIFF_DV + 1, :]
    o = acc[:, :tq] / l[:, :tq] - lam * (acc[:, tq:] / l[:, tq:])
    o = o * lax.rsqrt(jnp.mean(o * o, axis=0, keepdims=True) + EPS) * sw_ref[...] * (1.0 - lam_init)
    o_ref[...] = o.T.astype(BF16)


def _attention(lam_vec, subln_w, q, k, vt, *, tq, tk, q_block0, n_q, kv_block0, n_kv, lam_init):
    bsz = q.shape[0]
    n_chunks = n_kv // tk
    if tk == ATT_TK:
        assert kv_block0 == 0 and n_chunks == vt.shape[2]
        vt_spec = pl.BlockSpec((None, None, n_chunks, ATT_VROWS, tk), lambda b, h, i: (b, h, 0, 0, 0))
    else:
        assert n_chunks == 1 and (kv_block0 + 1) * tk == vt.shape[2] * ATT_TK
        vt_spec = pl.BlockSpec((None, None, 1, ATT_VROWS, tk),
                               lambda b, h, i: (b, h, vt.shape[2] - 1, 0, ATT_TK // tk - 1))
    kern = functools.partial(_attn_kernel, tq=tq, tk=tk, n_chunks=n_chunks, n_q=n_q, lam_init=lam_init)
    q_spec = lambda im: pl.BlockSpec((None, None, tq, DIFF_DV), im)
    return pl.pallas_call(
        kern,
        grid=(bsz, H_DIFF, n_q),
        in_specs=[
            pl.BlockSpec((4, DIFF_DH), lambda b, h, i: (0, 0)),
            pl.BlockSpec((DIFF_DV, 1), lambda b, h, i: (0, 0)),
            q_spec(lambda b, h, i: (b, h, q_block0 + i, 0)),
            q_spec(lambda b, h, i: (b, h, q_block0 + jnp.minimum(i + 1, n_q - 1), 0)),
            pl.BlockSpec((None, None, n_kv, DIFF_DV), lambda b, h, i: (b, h, kv_block0, 0)),
            vt_spec,
        ],
        out_specs=pl.BlockSpec((None, tq, DIFF_DV), lambda b, h, i: (b, i, h)),
        out_shape=jax.ShapeDtypeStruct((bsz, n_q * tq, H_DIFF * DIFF_DV), BF16),
        scratch_shapes=[
            pltpu.VMEM((2, DIFF_DV, 2 * tq), BF16),
            pltpu.VMEM((1, 2 * tq), F32),
            pltpu.VMEM((ATT_VROWS, 2 * tq), F32),
            pltpu.VMEM((2, tk, 2 * tq), F32),
            pltpu.VMEM((2, 1, 2 * tq), F32),
        ],
        compiler_params=_cparams(("arbitrary", "arbitrary", "arbitrary")),
        name="diffattn",
    )(lam_vec, subln_w.reshape(DIFF_DV, 1), q, q, k, vt)


def _merge_ffn_kernel(x_ref, modm_ref, modf_ref, nw_ref, gn_ref, yf_ref, yb_ref, rg_ref, al_ref, ac_ref, gt_ref,
                      wo_ref, win_ref, wout_ref, o_ref, xm_ref, *, ctx_tile, n_tiles):
    t = pl.program_id(1)

    @pl.when(t == 0)
    def _init():
        xm_ref[...] = jnp.zeros_like(xm_ref)

    xp = xm_ref[...]
    h = _rms(xp, nw_ref[4:5, :]) * (1.0 + modf_ref[7:8, :]) + modf_ref[6:7, :]
    hb = h.astype(BF16)
    a = _mm(hb, win_ref[:, :D_FF])
    b = _mm(hb, win_ref[:, D_FF:])
    u = (a * jax.nn.sigmoid(a) * b).astype(BF16)
    yf2 = _mm(u, wout_ref[...])
    o_ref[...] = xp + 0.5 * modf_ref[8:9, :] * _rms(yf2, nw_ref[5:6, :])

    x = x_ref[...]
    is_ctx = jnp.logical_and(t == ctx_tile, n_tiles > ctx_tile)
    diff = jnp.where(is_ctx, ac_ref[...], al_ref[...]).astype(F32)
    y = yf_ref[...].astype(F32) + yb_ref[...].astype(F32)
    parts = []
    for hd in range(H_RET):
        sl = slice(hd * RET_DV, (hd + 1) * RET_DV)
        yh = y[:, sl]
        mu = jnp.mean(yh, axis=-1, keepdims=True)
        yc = yh - mu
        var = jnp.mean(yc * yc, axis=-1, keepdims=True)
        parts.append(yc * lax.rsqrt(var + EPS) * gn_ref[:, sl])
    ret = jnp.concatenate(parts, axis=1) * rg_ref[...].astype(F32)
    ga = gt_ref[:, :D_MODEL].astype(F32)
    gb = gt_ref[:, D_MODEL:].astype(F32)
    z = (ga * ret + gb * diff).astype(BF16)
    yo = _mm(z, wo_ref[...])
    xm_ref[...] = x + modm_ref[5:6, :] * _rms(yo, nw_ref[3:4, :])


def _merge_ffn(xa, modsel, nw, gn_w, yf, yb, rg, at_lat, at_ctx, gates, w_out, ffn_w_in, ffn_w_out, *, ctx_tile, n_tiles):
    bsz = xa.shape[0]
    cur = lambda t: jnp.minimum(t, n_tiles - 1)
    prev = lambda t: jnp.maximum(t - 1, 0)
    tok = lambda width: pl.BlockSpec((None, TM, width), lambda b, t: (b, cur(t), 0))
    mod_spec = lambda sel: pl.BlockSpec((None, None, N_MOD, D_MODEL), lambda b, t: (b, sel(t) // ctx_tile, 0, 0))
    kern = functools.partial(_merge_ffn_kernel, ctx_tile=ctx_tile, n_tiles=n_tiles)
    return pl.pallas_call(
        kern,
        grid=(bsz, n_tiles + 1),
        in_specs=[
            tok(D_MODEL), mod_spec(cur), mod_spec(prev),
            _resident((6, D_MODEL), lambda b, t: (0, 0)),
            _resident((1, D_MODEL), lambda b, t: (0, 0)),
            tok(1024), tok(1024), tok(1024),
            pl.BlockSpec((None, TM, D_MODEL), lambda b, t: (b, jnp.minimum(t, ctx_tile - 1), 0)),
            pl.BlockSpec((None, TM, D_MODEL), lambda b, t: (b, 0, 0)),
            tok(2048),
            _resident((D_MODEL, D_MODEL), lambda b, t: (0, 0)),
            _resident((D_MODEL, 2 * D_FF), lambda b, t: (0, 0)),
            _resident((D_FF, D_MODEL), lambda b, t: (0, 0)),
        ],
        out_specs=pl.BlockSpec((None, TM, D_MODEL), lambda b, t: (b, prev(t), 0)),
        out_shape=jax.ShapeDtypeStruct((bsz, n_tiles * TM, D_MODEL), F32),
        scratch_shapes=[pltpu.VMEM((TM, D_MODEL), F32)],
        compiler_params=_cparams(("arbitrary", "arbitrary")),
        name="merge_ffn",
    )(xa, modsel, modsel, nw, gn_w, yf, yb, rg, at_lat, at_ctx, gates, w_out, ffn_w_in, ffn_w_out)


def _rope_tables(t, n):
    pos = jnp.arange(n, dtype=jnp.int32)
    is_lat = (pos < t)[:, None]
    lane = jnp.arange(128)
    inv_r = ROPE_BASE ** (-jnp.arange(64, dtype=F32) / 64)
    ang = pos.astype(F32)[:, None] * inv_r[None, :]
    cos, sin = jnp.cos(ang), jnp.sin(ang)
    cr = jnp.where(is_lat, jnp.concatenate([cos, cos], axis=1), 1.0)
    sr = jnp.where(is_lat, jnp.concatenate([-sin, sin], axis=1), 0.0)
    inv_d = ROPE_BASE ** (-jnp.arange(16, dtype=F32) / 16)
    row = (pos // GRID_W).astype(F32)[:, None] * inv_d[None, :]
    col = (pos % GRID_W).astype(F32)[:, None] * inv_d[None, :]
    cos64 = jnp.concatenate([jnp.cos(row)] * 2 + [jnp.cos(col)] * 2, axis=1)
    sin64 = jnp.concatenate([jnp.sin(row)] * 2 + [jnp.sin(col)] * 2, axis=1)
    cos128 = jnp.concatenate([cos64, cos64], axis=1)
    sin128 = jnp.concatenate([sin64, sin64], axis=1)
    upper = ((lane % 32) >= 16)[None, :]
    cd = jnp.where(is_lat, cos128, 1.0)
    sa = jnp.where(is_lat & upper, sin128, 0.0)
    sb = jnp.where(is_lat & ~upper, -sin128, 0.0)
    return cr, sr, cd, sa, sb


def kernel(x, c, ctx, c_ctx, w_ada, b_ada, norm_w, ffn_w_in, ffn_w_out, w_in, w_out,
           ret_decay_logit, ret_gn_w, diff_lambda, diff_subln_w):
    bsz, t, d = x.shape
    n_ctx = ctx.shape[1]
    depth = w_ada.shape[0]
    assert d == D_MODEL and n_ctx == TM and t % ATT_TQ == 0 and bsz + 1 <= 8
    n = t + n_ctx
    assert n % ATT_TK == 0
    n_tiles = n // TM
    ctx_tile = n_tiles - 1

    xa = jnp.concatenate([x, ctx], axis=1)
    cs = jnp.zeros((8, d), F32).at[:bsz].set(c).at[bsz].set(c_ctx)
    mod = _adaln(cs, w_ada, b_ada).reshape(depth, 8, N_MOD, d)
    modsel = jnp.stack([mod[:, :bsz], jnp.broadcast_to(mod[:, bsz:bsz + 1], (depth, bsz, N_MOD, d))], axis=2)
    tables = _rope_tables(t, n)

    cast = lambda w: w.astype(BF16)

    out = None
    for l in range(depth):
        last = l == depth - 1
        lam_init = 0.8 - 0.6 * math.exp(-0.3 * l)
        nw = norm_w[l]
        xa = _ffn(xa, modsel[l], nw, cast(ffn_w_in[l, 0]), cast(ffn_w_out[l, 0]),
                  mod_off=0, nw_off=0, n_tiles=n_tiles, ctx_tile=ctx_tile)
        qr, kr, vr, rg, qd, kd, vd, gates = _inproj(xa, modsel[l], nw, cast(w_in[l]), tables, ctx_tile=ctx_tile)
        yf, yb = _retention(ret_decay_logit[l], qr, kr, vr)
        at_lat = _attention(diff_lambda[l], diff_subln_w[l], qd, kd, vd, tq=ATT_TQ, tk=ATT_TK,
                            q_block0=0, n_q=t // ATT_TQ, kv_block0=0, n_kv=n, lam_init=lam_init)
        at_ctx = _attention(diff_lambda[l], diff_subln_w[l], qd, kd, vd, tq=TM, tk=TM,
                            q_block0=ctx_tile, n_q=1, kv_block0=ctx_tile, n_kv=TM, lam_init=lam_init)
        res = _merge_ffn(xa, modsel[l], nw, ret_gn_w[l].reshape(1, d), yf, yb, rg, at_lat, at_ctx, gates,
                         cast(w_out[l]), cast(ffn_w_in[l, 1]), cast(ffn_w_out[l, 1]),
                         ctx_tile=ctx_tile, n_tiles=(t // TM) if last else n_tiles)
        if last:
            out = res
        else:
            xa = res
    return out
```

```python
import functools
import math

import jax
import jax.numpy as jnp
from jax import lax
from jax.experimental import pallas as pl
from jax.experimental.pallas import tpu as pltpu

D_MODEL = 1024
D_FF = 2816
H_RET = 4
RET_DK = 128
RET_DV = 256
H_DIFF = 8
DIFF_DV = 128
DIFF_DH = 64
GRID_W = 64
ROPE_BASE = 10000.0
EPS = 1e-6
LOG2E = 1.4426950408889634
N_MOD = 9
IN_WIDTH = 8192
OFF_RQ, OFF_RK, OFF_RV, OFF_RG, OFF_DQ, OFF_DK, OFF_DV, OFF_GT = 0, 512, 1024, 2048, 3072, 4096, 5120, 6144

TM = 256
RET_BLOCK = 256
ATT_TQ = 512
ATT_TK = 1280
ATT_UNROLL = 6
ATT_RB = 256
ATT_VROWS = 144
VMEM_LIMIT = 56 * 1024 * 1024

BF16 = jnp.bfloat16
F32 = jnp.float32


def _cparams(sem):
    return pltpu.CompilerParams(dimension_semantics=sem, vmem_limit_bytes=VMEM_LIMIT)


def _resident(shape, index_map):
    return pl.BlockSpec(shape, index_map, pipeline_mode=pl.Buffered(1))


def _rms(x, g):
    return x * lax.rsqrt(jnp.mean(x * x, axis=-1, keepdims=True) + EPS) * g


def _mm(a, b):
    return jnp.dot(a, b, preferred_element_type=F32)


def _mm_nt(a, b):
    return lax.dot_general(a, b, (((1,), (1,)), ((), ())), preferred_element_type=F32)


def _mm_tn(a, b):
    return lax.dot_general(a, b, (((0,), (0,)), ((), ())), preferred_element_type=F32)


def _adaln_kernel(c_ref, w_ref, b_ref, o_ref):
    c = c_ref[...]
    s = c * jax.nn.sigmoid(c)
    o_ref[...] = _mm(s, w_ref[...]) + b_ref[...]


def _adaln(cs, w_ada, b_ada):
    depth = w_ada.shape[0]
    width = w_ada.shape[2]
    tn = 1024
    return pl.pallas_call(
        _adaln_kernel,
        grid=(depth, width // tn),
        in_specs=[
            pl.BlockSpec((8, D_MODEL), lambda l, j: (0, 0)),
            pl.BlockSpec((None, D_MODEL, tn), lambda l, j: (l, 0, j)),
            pl.BlockSpec((None, 1, tn), lambda l, j: (l, 0, j)),
        ],
        out_specs=pl.BlockSpec((None, 8, tn), lambda l, j: (l, 0, j)),
        out_shape=jax.ShapeDtypeStruct((depth, 8, width), F32),
        compiler_params=_cparams(("arbitrary", "arbitrary")),
        name="adaln",
    )(cs, w_ada, b_ada.reshape(depth, 1, width))


def _ffn_kernel(x_ref, mod_ref, nw_ref, win_ref, wout_ref, o_ref, *, mod_off, nw_off):
    x = x_ref[...]
    shift = mod_ref[mod_off:mod_off + 1, :]
    scale = mod_ref[mod_off + 1:mod_off + 2, :]
    gate = mod_ref[mod_off + 2:mod_off + 3, :]
    h = _rms(x, nw_ref[nw_off:nw_off + 1, :]) * (1.0 + scale) + shift
    hb = h.astype(BF16)
    a = _mm(hb, win_ref[:, :D_FF])
    b = _mm(hb, win_ref[:, D_FF:])
    u = (a * jax.nn.sigmoid(a) * b).astype(BF16)
    y = _mm(u, wout_ref[...])
    o_ref[...] = x + 0.5 * gate * _rms(y, nw_ref[nw_off + 1:nw_off + 2, :])


def _ffn(xa, modsel, nw, w_in, w_out, *, mod_off, nw_off, n_tiles, ctx_tile):
    bsz = xa.shape[0]
    kern = functools.partial(_ffn_kernel, mod_off=mod_off, nw_off=nw_off)
    return pl.pallas_call(
        kern,
        grid=(bsz, n_tiles),
        in_specs=[
            pl.BlockSpec((None, TM, D_MODEL), lambda b, t: (b, t, 0)),
            pl.BlockSpec((None, None, N_MOD, D_MODEL), lambda b, t: (b, t // ctx_tile, 0, 0)),
            _resident((6, D_MODEL), lambda b, t: (0, 0)),
            _resident((D_MODEL, 2 * D_FF), lambda b, t: (0, 0)),
            _resident((D_FF, D_MODEL), lambda b, t: (0, 0)),
        ],
        out_specs=pl.BlockSpec((None, TM, D_MODEL), lambda b, t: (b, t, 0)),
        out_shape=jax.ShapeDtypeStruct((bsz, n_tiles * TM, D_MODEL), F32),
        compiler_params=_cparams(("arbitrary", "arbitrary")),
        name="ffn",
    )(xa, modsel, nw, w_in, w_out)


def _inproj_kernel(x_ref, mod_ref, nw_ref, w_ref, cr_ref, sr_ref, cd_ref, sa_ref, sb_ref,
                   qr_ref, kr_ref, vr_ref, rg_ref, qd_ref, kd_ref, vd_ref, gt_ref):
    x = x_ref[...]
    h = _rms(x, nw_ref[2:3, :]) * (1.0 + mod_ref[4:5, :]) + mod_ref[3:4, :]
    hb = h.astype(BF16)
    cr, sr = cr_ref[...], sr_ref[...]
    cd, sa, sb = cd_ref[...], sa_ref[...], sb_ref[...]

    rq = _mm(hb, w_ref[:, OFF_RQ:OFF_RK])
    rk = _mm(hb, w_ref[:, OFF_RK:OFF_RV])
    for hd in range(H_RET):
        sl = slice(hd * RET_DK, (hd + 1) * RET_DK)
        q = rq[:, sl]
        k = rk[:, sl]
        qr_ref[:, sl] = (q * cr + pltpu.roll(q, 64, 1) * sr).astype(BF16)
        kr_ref[:, sl] = ((k * cr + pltpu.roll(k, 64, 1) * sr) * (RET_DK ** -0.5)).astype(BF16)
    vr_ref[...] = _mm(hb, w_ref[:, OFF_RV:OFF_RG]).astype(BF16)
    rg = _mm(hb, w_ref[:, OFF_RG:OFF_DQ])
    rg_ref[...] = (rg * jax.nn.sigmoid(rg)).astype(BF16)

    dq = _mm(hb, w_ref[:, OFF_DQ:OFF_DK])
    dk = _mm(hb, w_ref[:, OFF_DK:OFF_DV])
    dv = _mm(hb, w_ref[:, OFF_DV:OFF_GT])
    for hd in range(H_DIFF):
        sl = slice(hd * DIFF_DV, (hd + 1) * DIFF_DV)
        q = dq[:, sl]
        k = dk[:, sl]
        qrot = q * cd + pltpu.roll(q, 16, 1) * sa + pltpu.roll(q, 112, 1) * sb
        krot = k * cd + pltpu.roll(k, 16, 1) * sa + pltpu.roll(k, 112, 1) * sb
        qd_ref[hd] = (qrot * (DIFF_DH ** -0.5 * LOG2E)).astype(BF16)
        kd_ref[hd] = krot.astype(BF16)
        vd_ref[hd, 0:DIFF_DV, :] = dv[:, sl].T.astype(BF16)
        vd_ref[hd, DIFF_DV:ATT_VROWS, :] = jnp.ones((ATT_VROWS - DIFF_DV, TM), BF16)
    gt_ref[...] = jax.nn.sigmoid(_mm(hb, w_ref[:, OFF_GT:IN_WIDTH])).astype(BF16)


def _inproj(xa, modsel, nw, w_in, tables, *, ctx_tile):
    bsz, n, _ = xa.shape
    n_tiles = n // TM
    tok = lambda width: pl.BlockSpec((None, TM, width), lambda b, t: (b, t, 0))
    tab = pl.BlockSpec((TM, 128), lambda b, t: (t, 0))
    heads = pl.BlockSpec((None, H_DIFF, TM, DIFF_DV), lambda b, t: (b, 0, t, 0))
    tiles_per_chunk = ATT_TK // TM
    heads_t = pl.BlockSpec((None, H_DIFF, None, ATT_VROWS, TM),
                           lambda b, t: (b, 0, t // tiles_per_chunk, 0, t % tiles_per_chunk))
    sds = jax.ShapeDtypeStruct
    return pl.pallas_call(
        _inproj_kernel,
        grid=(bsz, n_tiles),
        in_specs=[
            tok(D_MODEL),
            pl.BlockSpec((None, None, N_MOD, D_MODEL), lambda b, t: (b, t // ctx_tile, 0, 0)),
            _resident((6, D_MODEL), lambda b, t: (0, 0)),
            _resident((D_MODEL, IN_WIDTH), lambda b, t: (0, 0)),
            tab, tab, tab, tab, tab,
        ],
        out_specs=[tok(512), tok(512), tok(1024), tok(1024), heads, heads, heads_t, tok(2048)],
        out_shape=[
            sds((bsz, n, 512), BF16), sds((bsz, n, 512), BF16),
            sds((bsz, n, 1024), BF16), sds((bsz, n, 1024), BF16),
            sds((bsz, H_DIFF, n, DIFF_DV), BF16), sds((bsz, H_DIFF, n, DIFF_DV), BF16),
            sds((bsz, H_DIFF, n // ATT_TK, ATT_VROWS, ATT_TK), BF16), sds((bsz, n, 2048), BF16),
        ],
        compiler_params=_cparams(("arbitrary", "arbitrary")),
        name="inproj",
    )(xa, modsel, nw, w_in, *tables)


def _retention_kernel(dl_ref, qf_ref, kf_ref, vf_ref, qb_ref, kb_ref, vb_ref, yf_ref, yb_ref,
                      st_ref, dm_ref, zt_ref, xi_ref):
    c = RET_BLOCK
    dl = dl_ref[...]
    lg = jnp.minimum(dl, 0.0) - jnp.log(1.0 + jnp.exp(-jnp.abs(dl)))

    @pl.when(pl.program_id(0) == 0)
    def _init():
        st_ref[...] = jnp.zeros_like(st_ref)
        ri = lax.broadcasted_iota(jnp.int32, (c, c), 0).astype(F32)
        ci = lax.broadcasted_iota(jnp.int32, (c, c), 1).astype(F32)
        pos = lax.broadcasted_iota(jnp.int32, (c, RET_DK), 0).astype(F32)
        for d in range(2):
            diff = ri - ci if d == 0 else ci - ri
            for hd in range(H_RET):
                g = lg[d:d + 1, hd:hd + 1]
                dm_ref[d, hd] = jnp.where(diff >= 0.0, jnp.exp(jnp.maximum(diff, 0.0) * g), 0.0)
                if d == 0:
                    zt_ref[d, hd] = jnp.exp((c - 1.0 - pos) * g)
                    xi_ref[d, hd] = jnp.exp((pos + 1.0) * g)
                else:
                    zt_ref[d, hd] = jnp.exp(pos * g)
                    xi_ref[d, hd] = jnp.exp((c - pos) * g)

    for d, (q_ref, k_ref, v_ref, y_ref) in enumerate(
            ((qf_ref, kf_ref, vf_ref, yf_ref), (qb_ref, kb_ref, vb_ref, yb_ref))):
        for hd in range(H_RET):
            g_blk = jnp.exp(float(c) * lg[d:d + 1, hd:hd + 1])
            for b in range(q_ref.shape[0]):
                q = q_ref[b, :, hd * RET_DK:(hd + 1) * RET_DK]
                k = k_ref[b, :, hd * RET_DK:(hd + 1) * RET_DK]
                v = v_ref[b, :, hd * RET_DV:(hd + 1) * RET_DV]
                s = _mm_nt(q, k) * dm_ref[d, hd]
                inner = _mm(s.astype(BF16), v)
                state = st_ref[b, d, hd]
                qx = (q.astype(F32) * xi_ref[d, hd]).astype(BF16)
                cross = _mm(qx, state.astype(BF16))
                y_ref[b, :, hd * RET_DV:(hd + 1) * RET_DV] = (inner + cross).astype(BF16)
                kz = (k.astype(F32) * zt_ref[d, hd]).astype(BF16)
                st_ref[b, d, hd] = g_blk * state + _mm_tn(kz, v)


def _retention(decay_logit, q, k, v):
    bsz, n, _ = q.shape
    nb = n // RET_BLOCK
    last = nb - 1
    fwd = lambda i: (0, jnp.where(i == 0, last, i - 1), 0)
    bwd = lambda i: (0, jnp.where(i == 0, last, last - i), 0)
    blk = lambda width, im: pl.BlockSpec((bsz, RET_BLOCK, width), im)
    y = jax.ShapeDtypeStruct((bsz, n, H_RET * RET_DV), BF16)
    return pl.pallas_call(
        _retention_kernel,
        grid=(nb,),
        in_specs=[
            pl.BlockSpec((2, H_RET), lambda i: (0, 0)),
            blk(512, fwd), blk(512, fwd), blk(1024, fwd),
            blk(512, bwd), blk(512, bwd), blk(1024, bwd),
        ],
        out_specs=[blk(1024, fwd), blk(1024, bwd)],
        out_shape=[y, y],
        scratch_shapes=[
            pltpu.VMEM((bsz, 2, H_RET, RET_DK, RET_DV), F32),
            pltpu.VMEM((2, H_RET, RET_BLOCK, RET_BLOCK), F32),
            pltpu.VMEM((2, H_RET, RET_BLOCK, RET_DK), F32),
            pltpu.VMEM((2, H_RET, RET_BLOCK, RET_DK), F32),
        ],
        compiler_params=_cparams(("arbitrary",)),
        name="retention",
    )(decay_logit, q, k, v, q, k, v)


def _attn_kernel(lam_ref, sw_ref, q_ref, qn_ref, k_ref, vt_ref, o_ref, qq_ref, m_ref, acc_ref, s_ref, cm_ref,
                 *, tq, tk, n_chunks, n_q, lam_init):
    i_q = pl.program_id(2)
    n_rb = tk // ATT_RB

    def build_qq(src_ref, par):
        qt = src_ref[...].astype(F32).T
        row = lax.broadcasted_iota(jnp.int32, qt.shape, 0)
        qq_ref[par, :, :tq] = jnp.where(row < DIFF_DH, qt, 0.0).astype(BF16)
        qq_ref[par, :, tq:] = jnp.where(row >= DIFF_DH, qt, 0.0).astype(BF16)

    def scores(ci, slot, qpar):
        start = pl.multiple_of(ci * tk, tk)
        s = _mm(k_ref[pl.ds(start, tk), :], qq_ref[qpar])
        s_ref[slot] = s
        cm_ref[slot] = jnp.max(s, axis=0, keepdims=True)

    def softmax_pv(ci, slot):
        s = s_ref[slot]
        m_old = m_ref[...]
        m_new = jnp.maximum(m_old, cm_ref[slot])
        alpha = jnp.exp2(m_old - m_new)
        p = jnp.exp2(s - m_new)
        acc_ref[...] = alpha * acc_ref[...] + _mm(vt_ref[ci], p.astype(BF16))
        m_ref[...] = m_new

    def fused(c_next, c_cur, slot, qpar_next):
        nstart = pl.multiple_of(c_next * tk, tk)
        m_old = m_ref[...]
        m_new = jnp.maximum(m_old, cm_ref[slot])
        alpha = jnp.exp2(m_old - m_new)
        pv = None
        cmax = None
        for r in range(n_rb):
            rows = slice(r * ATT_RB, (r + 1) * ATT_RB)
            sn = _mm(k_ref[pl.ds(nstart + r * ATT_RB, ATT_RB), :], qq_ref[qpar_next])
            s_ref[1 - slot, rows, :] = sn
            cm_r = jnp.max(sn.reshape(ATT_RB // 8, 8, 2 * tq), axis=0)
            cmax = cm_r if cmax is None else jnp.maximum(cmax, cm_r)
            p = jnp.exp2(s_ref[slot, rows, :] - m_new)
            pv_r = _mm(vt_ref[c_cur, :, rows], p.astype(BF16))
            pv = pv_r if pv is None else pv + pv_r
        cm_ref[1 - slot] = jnp.max(cmax, axis=0, keepdims=True)
        acc_ref[...] = alpha * acc_ref[...] + pv
        m_ref[...] = m_new

    def tile(par):
        @pl.when(i_q == 0)
        def _first_tile():
            build_qq(q_ref, par)
            scores(0, par, par)

        m_ref[...] = jnp.full_like(m_ref, -jnp.inf)
        acc_ref[...] = jnp.zeros_like(acc_ref)

        n_steady = n_chunks - 1
        n_loop = n_steady // ATT_UNROLL

        def body(j, carry):
            c0 = ATT_UNROLL * j
            for u in range(ATT_UNROLL):
                fused(c0 + u + 1, c0 + u, (u + par) % 2, par)
            return carry

        if n_loop > 0:
            lax.fori_loop(0, n_loop, body, 0)
        for c in range(n_loop * ATT_UNROLL, n_steady):
            fused(c + 1, c, (c + par) % 2, par)

        last = n_chunks - 1
        last_slot = (last + par) % 2

        @pl.when(i_q + 1 < n_q)
        def _chain():
            build_qq(qn_ref, 1 - par)
            if last_slot == par:
                fused(0, last, last_slot, 1 - par)
            else:
                softmax_pv(last, last_slot)
                scores(0, 1 - par, 1 - par)

        @pl.when(i_q + 1 >= n_q)
        def _end():
            softmax_pv(last, last_slot)

    if n_q == 1:
        tile(0)
    else:
        @pl.when(i_q % 2 == 0)
        def _even():
            tile(0)

        @pl.when(i_q % 2 == 1)
        def _odd():
            tile(1)

    lv = lam_ref[...]
    lam = (jnp.exp(jnp.sum(lv[0:1, :] * lv[1:2, :], axis=1, keepdims=True))
           - jnp.exp(jnp.sum(lv[2:3, :] * lv[3:4, :], axis=1, keepdims=True)) + lam_init)
    acc = acc_ref[0:DIFF_DV, :]
    l = acc_ref[DIFF_DV:DIFF_DV + 1, :]
    o = acc[:, :tq] / l[:, :tq] - lam * (acc[:, tq:] / l[:, tq:])
    o = o * lax.rsqrt(jnp.mean(o * o, axis=0, keepdims=True) + EPS) * sw_ref[...] * (1.0 - lam_init)
    o_ref[...] = o.T.astype(BF16)


def _attention(lam_vec, subln_w, q, k, vt, *, tq, tk, q_block0, n_q, kv_block0, n_kv, lam_init):
    bsz = q.shape[0]
    n_chunks = n_kv // tk
    if tk == ATT_TK:
        assert kv_block0 == 0 and n_chunks == vt.shape[2]
        vt_spec = pl.BlockSpec((None, None, n_chunks, ATT_VROWS, tk), lambda b, h, i: (b, h, 0, 0, 0))
    else:
        assert n_chunks == 1 and (kv_block0 + 1) * tk == vt.shape[2] * ATT_TK
        vt_spec = pl.BlockSpec((None, None, 1, ATT_VROWS, tk),
                               lambda b, h, i: (b, h, vt.shape[2] - 1, 0, ATT_TK // tk - 1))
    kern = functools.partial(_attn_kernel, tq=tq, tk=tk, n_chunks=n_chunks, n_q=n_q, lam_init=lam_init)
    q_spec = lambda im: pl.BlockSpec((None, None, tq, DIFF_DV), im)
    return pl.pallas_call(
        kern,
        grid=(bsz, H_DIFF, n_q),
        in_specs=[
            pl.BlockSpec((4, DIFF_DH), lambda b, h, i: (0, 0)),
            pl.BlockSpec((DIFF_DV, 1), lambda b, h, i: (0, 0)),
            q_spec(lambda b, h, i: (b, h, q_block0 + i, 0)),
            q_spec(lambda b, h, i: (b, h, q_block0 + jnp.minimum(i + 1, n_q - 1), 0)),
            pl.BlockSpec((None, None, n_kv, DIFF_DV), lambda b, h, i: (b, h, kv_block0, 0)),
            vt_spec,
        ],
        out_specs=pl.BlockSpec((None, tq, DIFF_DV), lambda b, h, i: (b, i, h)),
        out_shape=jax.ShapeDtypeStruct((bsz, n_q * tq, H_DIFF * DIFF_DV), BF16),
        scratch_shapes=[
            pltpu.VMEM((2, DIFF_DV, 2 * tq), BF16),
            pltpu.VMEM((1, 2 * tq), F32),
            pltpu.VMEM((ATT_VROWS, 2 * tq), F32),
            pltpu.VMEM((2, tk, 2 * tq), F32),
            pltpu.VMEM((2, 1, 2 * tq), F32),
        ],
        compiler_params=_cparams(("arbitrary", "arbitrary", "arbitrary")),
        name="diffattn",
    )(lam_vec, subln_w.reshape(DIFF_DV, 1), q, q, k, vt)


def _merge_ffn_kernel(x_ref, modm_ref, modf_ref, nw_ref, gn_ref, yf_ref, yb_ref, rg_ref, al_ref, ac_ref, gt_ref,
                      wo_ref, win_ref, wout_ref, o_ref, xm_ref, *, ctx_tile, n_tiles):
    t = pl.program_id(1)

    @pl.when(t == 0)
    def _init():
        xm_ref[...] = jnp.zeros_like(xm_ref)

    xp = xm_ref[...]
    h = _rms(xp, nw_ref[4:5, :]) * (1.0 + modf_ref[7:8, :]) + modf_ref[6:7, :]
    hb = h.astype(BF16)
    a = _mm(hb, win_ref[:, :D_FF])
    b = _mm(hb, win_ref[:, D_FF:])
    u = (a * jax.nn.sigmoid(a) * b).astype(BF16)
    yf2 = _mm(u, wout_ref[...])
    o_ref[...] = xp + 0.5 * modf_ref[8:9, :] * _rms(yf2, nw_ref[5:6, :])

    x = x_ref[...]
    is_ctx = jnp.logical_and(t == ctx_tile, n_tiles > ctx_tile)
    diff = jnp.where(is_ctx, ac_ref[...], al_ref[...]).astype(F32)
    y = yf_ref[...].astype(F32) + yb_ref[...].astype(F32)
    parts = []
    for hd in range(H_RET):
        sl = slice(hd * RET_DV, (hd + 1) * RET_DV)
        yh = y[:, sl]
        mu = jnp.mean(yh, axis=-1, keepdims=True)
        yc = yh - mu
        var = jnp.mean(yc * yc, axis=-1, keepdims=True)
        parts.append(yc * lax.rsqrt(var + EPS) * gn_ref[:, sl])
    ret = jnp.concatenate(parts, axis=1) * rg_ref[...].astype(F32)
    ga = gt_ref[:, :D_MODEL].astype(F32)
    gb = gt_ref[:, D_MODEL:].astype(F32)
    z = (ga * ret + gb * diff).astype(BF16)
    yo = _mm(z, wo_ref[...])
    xm_ref[...] = x + modm_ref[5:6, :] * _rms(yo, nw_ref[3:4, :])


def _merge_ffn(xa, modsel, nw, gn_w, yf, yb, rg, at_lat, at_ctx, gates, w_out, ffn_w_in, ffn_w_out, *, ctx_tile, n_tiles):
    bsz = xa.shape[0]
    cur = lambda t: jnp.minimum(t, n_tiles - 1)
    prev = lambda t: jnp.maximum(t - 1, 0)
    tok = lambda width: pl.BlockSpec((None, TM, width), lambda b, t: (b, cur(t), 0))
    mod_spec = lambda sel: pl.BlockSpec((None, None, N_MOD, D_MODEL), lambda b, t: (b, sel(t) // ctx_tile, 0, 0))
    kern = functools.partial(_merge_ffn_kernel, ctx_tile=ctx_tile, n_tiles=n_tiles)
    return pl.pallas_call(
        kern,
        grid=(bsz, n_tiles + 1),
        in_specs=[
            tok(D_MODEL), mod_spec(cur), mod_spec(prev),
            _resident((6, D_MODEL), lambda b, t: (0, 0)),
            _resident((1, D_MODEL), lambda b, t: (0, 0)),
            tok(1024), tok(1024), tok(1024),
            pl.BlockSpec((None, TM, D_MODEL), lambda b, t: (b, jnp.minimum(t, ctx_tile - 1), 0)),
            pl.BlockSpec((None, TM, D_MODEL), lambda b, t: (b, 0, 0)),
            tok(2048),
            _resident((D_MODEL, D_MODEL), lambda b, t: (0, 0)),
            _resident((D_MODEL, 2 * D_FF), lambda b, t: (0, 0)),
            _resident((D_FF, D_MODEL), lambda b, t: (0, 0)),
        ],
        out_specs=pl.BlockSpec((None, TM, D_MODEL), lambda b, t: (b, prev(t), 0)),
        out_shape=jax.ShapeDtypeStruct((bsz, n_tiles * TM, D_MODEL), F32),
        scratch_shapes=[pltpu.VMEM((TM, D_MODEL), F32)],
        compiler_params=_cparams(("arbitrary", "arbitrary")),
        name="merge_ffn",
    )(xa, modsel, modsel, nw, gn_w, yf, yb, rg, at_lat, at_ctx, gates, w_out, ffn_w_in, ffn_w_out)


def _rope_tables(t, n):
    pos = jnp.arange(n, dtype=jnp.int32)
    is_lat = (pos < t)[:, None]
    lane = jnp.arange(128)
    inv_r = ROPE_BASE ** (-jnp.arange(64, dtype=F32) / 64)
    ang = pos.astype(F32)[:, None] * inv_r[None, :]
    cos, sin = jnp.cos(ang), jnp.sin(ang)
    cr = jnp.where(is_lat, jnp.concatenate([cos, cos], axis=1), 1.0)
    sr = jnp.where(is_lat, jnp.concatenate([-sin, sin], axis=1), 0.0)
    inv_d = ROPE_BASE ** (-jnp.arange(16, dtype=F32) / 16)
    row = (pos // GRID_W).astype(F32)[:, None] * inv_d[None, :]
    col = (pos % GRID_W).astype(F32)[:, None] * inv_d[None, :]
    cos64 = jnp.concatenate([jnp.cos(row)] * 2 + [jnp.cos(col)] * 2, axis=1)
    sin64 = jnp.concatenate([jnp.sin(row)] * 2 + [jnp.sin(col)] * 2, axis=1)
    cos128 = jnp.concatenate([cos64, cos64], axis=1)
    sin128 = jnp.concatenate([sin64, sin64], axis=1)
    upper = ((lane % 32) >= 16)[None, :]
    cd = jnp.where(is_lat, cos128, 1.0)
    sa = jnp.where(is_lat & upper, sin128, 0.0)
    sb = jnp.where(is_lat & ~upper, -sin128, 0.0)
    return cr, sr, cd, sa, sb


def kernel(x, c, ctx, c_ctx, w_ada, b_ada, norm_w, ffn_w_in, ffn_w_out, w_in, w_out,
           ret_decay_logit, ret_gn_w, diff_lambda, diff_subln_w):
    bsz, t, d = x.shape
    n_ctx = ctx.shape[1]
    depth = w_ada.shape[0]
    assert d == D_MODEL and n_ctx == TM and t % ATT_TQ == 0 and bsz + 1 <= 8
    n = t + n_ctx
    assert n % ATT_TK == 0
    n_tiles = n // TM
    ctx_tile = n_tiles - 1

    xa = jnp.concatenate([x, ctx], axis=1)
    cs = jnp.zeros((8, d), F32).at[:bsz].set(c).at[bsz].set(c_ctx)
    mod = _adaln(cs, w_ada, b_ada).reshape(depth, 8, N_MOD, d)
    modsel = jnp.stack([mod[:, :bsz], jnp.broadcast_to(mod[:, bsz:bsz + 1], (depth, bsz, N_MOD, d))], axis=2)
    tables = _rope_tables(t, n)

    cast = lambda w: w.astype(BF16)

    out = None
    for l in range(depth):
        last = l == depth - 1
        lam_init = 0.8 - 0.6 * math.exp(-0.3 * l)
        nw = norm_w[l]
        xa = _ffn(xa, modsel[l], nw, cast(ffn_w_in[l, 0]), cast(ffn_w_out[l, 0]),
                  mod_off=0, nw_off=0, n_tiles=n_tiles, ctx_tile=ctx_tile)
        qr, kr, vr, rg, qd, kd, vd, gates = _inproj(xa, modsel[l], nw, cast(w_in[l]), tables, ctx_tile=ctx_tile)
        yf, yb = _retention(ret_decay_logit[l], qr, kr, vr)
        at_lat = _attention(diff_lambda[l], diff_subln_w[l], qd, kd, vd, tq=ATT_TQ, tk=ATT_TK,
                            q_block0=0, n_q=t // ATT_TQ, kv_block0=0, n_kv=n, lam_init=lam_init)
        at_ctx = _attention(diff_lambda[l], diff_subln_w[l], qd, kd, vd, tq=TM, tk=TM,
                            q_block0=ctx_tile, n_q=1, kv_block0=ctx_tile, n_kv=TM, lam_init=lam_init)
        res = _merge_ffn(xa, modsel[l], nw, ret_gn_w[l].reshape(1, d), yf, yb, rg, at_lat, at_ctx, gates,
                         cast(w_out[l]), cast(ffn_w_in[l, 1]), cast(ffn_w_out[l, 1]),
                         ctx_tile=ctx_tile, n_tiles=(t // TM) if last else n_tiles)
        if last:
            out = res
        else:
            xa = res
    return out
```
